```python
import numpy as np
import jax
import jax.numpy as jnp
from jax import lax

D_MODEL = 2048
BATCH = 16
SEQ = 2048
DEPTH = 2

HEAD_DIM = 128
A_HEADS = D_MODEL // (2 * HEAD_DIM)
B_HEADS = D_MODEL // (2 * HEAD_DIM)
C_HEADS = D_MODEL // (2 * HEAD_DIM)
C_KV_HEADS = C_HEADS // 4
D_HEADS = D_MODEL // (2 * HEAD_DIM)
A_WIDTH = A_HEADS * HEAD_DIM
B_WIDTH = B_HEADS * HEAD_DIM
C_WIDTH = C_HEADS * HEAD_DIM
D_WIDTH = D_HEADS * HEAD_DIM
KV_WIDTH = C_KV_HEADS * HEAD_DIM
A_CONV = 4
MLSTM_CHUNK = 64
SB_BLOCK = 128
CMP_LEN = 32
CMP_STRIDE = 16
SEL_LEN = 64
N_SEL = 8
N_LOCAL = 2
FORCE_BONUS = 1000.0
SEL_Q_BLOCK = 32
WINDOW = 256
WIN_BLOCK = 128
HGRN_CHUNK = 32
FFN_DIM = 256 * ((8 * D_MODEL // 3 + 255) // 256)
FFN_CONV = 3
ROPE_THETA = 500000.0
ROPE_DIM = HEAD_DIM // 4
RMS_EPS = 1e-6

kernel_name = 'hybrid_mlstm_stickbreak_nsa_hgrn2_block'


def rms_norm(x, g):
    xf = x.astype(jnp.float32)
    y = xf * lax.rsqrt(jnp.mean(xf * xf, axis=-1, keepdims=True) + RMS_EPS)
    return (y * g.astype(jnp.float32)).astype(x.dtype)


def head_rms_norm(a, g):
    return rms_norm(a, g.reshape(a.shape[1], 1, a.shape[3]))


def split_heads(a, n):
    b, s, _ = a.shape
    return a.reshape(b, s, n, -1).transpose(0, 2, 1, 3)


def merge_heads(a):
    b, h, s, d = a.shape
    return a.transpose(0, 2, 1, 3).reshape(b, s, h * d)


def split_cols(p, sizes):
    return jnp.split(p, [int(c) for c in np.cumsum(sizes)[:-1]], axis=-1)


def causal_dwconv(x, w, b):
    k = w.shape[0]
    y = lax.conv_general_dilated(x, w[:, None, :].astype(x.dtype), window_strides=(1,),
                                 padding=[(k - 1, 0)], dimension_numbers=('NWC', 'WIO', 'NWC'),
                                 feature_group_count=x.shape[-1])
    return y + b.astype(x.dtype)


def rope_partial(x, pos):
    half = ROPE_DIM // 2
    freqs = ROPE_THETA ** (-jnp.arange(half, dtype=jnp.float32) / half)
    ang = jnp.asarray(pos).astype(jnp.float32)[:, None] * freqs
    cos, sin = jnp.cos(ang), jnp.sin(ang)
    xr = x[..., :ROPE_DIM].astype(jnp.float32)
    x1, x2 = xr[..., :half], xr[..., half:]
    rot = jnp.concatenate([x1 * cos - x2 * sin, x2 * cos + x1 * sin], axis=-1).astype(x.dtype)
    return jnp.concatenate([rot, x[..., ROPE_DIM:]], axis=-1)


def masked_softmax(s, mask):
    s = jnp.where(mask, s.astype(jnp.float32), -jnp.inf)
    m = jnp.max(s, axis=-1, keepdims=True)
    m = jnp.where(jnp.isfinite(m), m, 0.0)
    e = jnp.exp(s - m)
    return e / jnp.maximum(jnp.sum(e, axis=-1, keepdims=True), 1e-30)


def mlstm_chunkwise(q, k, v, i_pre, f_pre):
    b_, h_, s_, d = q.shape
    L = min(MLSTM_CHUNK, s_)
    nc = s_ // L
    f32 = jnp.float32
    q = q.astype(f32)
    k = k.astype(f32) * (d ** -0.5)
    v = v.astype(f32)
    log_i = i_pre.astype(f32)
    log_f = jax.nn.log_sigmoid(f_pre.astype(f32))

    def chunk(a):
        return jnp.moveaxis(a.reshape((b_, h_, nc, L) + a.shape[3:]), 2, 0)

    causal = np.tril(np.ones((L, L), dtype=bool))

    def step(carry, inp):
        C, n, m = carry
        qc, kc, vc, ic, fc = inp
        b = jnp.cumsum(fc, axis=-1)
        g = b + m[..., None]
        dmat = jnp.where(causal, b[..., :, None] - b[..., None, :] + ic[..., None, :], -jnp.inf)
        m_row = jnp.maximum(g, jnp.max(dmat, axis=-1))
        w = jnp.exp(dmat - m_row[..., None]) * jnp.einsum('bhtd,bhsd->bhts', qc, kc)
        w_state = jnp.exp(g - m_row)
        num = (w_state[..., None] * jnp.einsum('bhtd,bhde->bhte', qc, C)
               + jnp.einsum('bhts,bhse->bhte', w, vc))
        den = w_state * jnp.einsum('bhtd,bhd->bht', qc, n) + jnp.sum(w, axis=-1)
        h = num / jnp.maximum(jnp.abs(den), jnp.exp(-m_row))[..., None]
        b_last = b[..., -1]
        w_new = b_last[..., None] - b + ic
        m_new = jnp.maximum(b_last + m, jnp.max(w_new, axis=-1))
        decay = jnp.exp(b_last + m - m_new)
        kw = jnp.exp(w_new - m_new[..., None])[..., None] * kc
        C = decay[..., None, None] * C + jnp.einsum('bhsd,bhse->bhde', kw, vc)
        n = decay[..., None] * n + jnp.sum(kw, axis=-2)
        return (C, n, m_new), h

    init = (jnp.zeros((b_, h_, d, v.shape[-1]), f32), jnp.zeros((b_, h_, d), f32),
            jnp.zeros((b_, h_), f32))
    _, hs = lax.scan(step, init, (chunk(q), chunk(k), chunk(v), chunk(log_i), chunk(log_f)))
    return jnp.moveaxis(hs, 0, 2).reshape(b_, h_, s_, -1)


def stick_breaking(q, k, v):
    s_, d = q.shape[2], q.shape[3]
    qb_size = min(SB_BLOCK, s_)
    outs = []
    for blk in range(s_ // qb_size):
        t0, t1 = blk * qb_size, (blk + 1) * qb_size
        z = jnp.einsum('bhtd,bhsd->bhts', q[:, :, t0:t1], k[:, :, :t1]).astype(jnp.float32) * (d ** -0.5)
        strict = np.arange(t1)[None, :] < np.arange(t0, t1)[:, None]
        log_keep = jnp.where(strict, jax.nn.log_sigmoid(-z), 0.0)
        later = lax.cumsum(log_keep, axis=3, reverse=True) - log_keep
        a = jnp.where(strict, jnp.exp(jax.nn.log_sigmoid(z) + later), 0.0)
        outs.append(jnp.einsum('bhts,bhse->bhte', a.astype(v.dtype), v[:, :, :t1]))
    return jnp.concatenate(outs, axis=2)


def nsa_attention(q, k_cmp, v_cmp, k_slc, v_slc, k_win, v_win, gates, cmp_pos, cmp_w1, cmp_w2):
    b_, g_, r_, s_, d = q.shape
    scale = d ** -0.5
    t = np.arange(s_)
    n_cmp = (s_ - CMP_LEN) // CMP_STRIDE + 1
    cidx = np.arange(n_cmp)[:, None] * CMP_STRIDE + np.arange(CMP_LEN)[None, :]
    cend = cidx[:, -1]

    def compress(a, pos_emb, w1, w2):
        blocks = a[:, :, cidx] + pos_emb
        hid = jax.nn.gelu(jnp.einsum('bgnld,lde->bgne', blocks, w1))
        return jnp.einsum('bgne,ef->bgnf', hid, w2)

    kc = rope_partial(compress(k_cmp, cmp_pos[0], cmp_w1[0], cmp_w2[0]), cend)
    vc = compress(v_cmp, cmp_pos[1], cmp_w1[1], cmp_w2[1])
    s_c = jnp.einsum('bgrtd,bgnd->bgrtn', q, kc).astype(jnp.float32) * scale
    p_cmp = masked_softmax(s_c, cend[None, :] <= t[:, None])
    o_cmp = jnp.einsum('bgrtn,bgnd->bgrtd', p_cmp.astype(vc.dtype), vc)

    n_sel = s_ // SEL_LEN
    cs = np.arange(n_cmp) * CMP_STRIDE
    ss = np.arange(n_sel) * SEL_LEN
    cover = ((cs[:, None] < ss[None, :] + SEL_LEN) & (cs[:, None] + CMP_LEN > ss[None, :])).astype(np.float32)
    p_slc = jnp.einsum('bgrtn,nj->bgtj', p_cmp, cover)
    blk_t = t // SEL_LEN
    jb = np.arange(n_sel)
    started = jb[None, :] <= blk_t[:, None]
    forced = (jb[None, :] == 0) | (started & (blk_t[:, None] - jb[None, :] < N_LOCAL))
    score = jnp.where(started, p_slc + jnp.where(forced, FORCE_BONUS, 0.0), -jnp.inf)
    n_top = min(N_SEL, n_sel)
    _, sel_idx = lax.top_k(score, n_top)

    qb_size = min(SEL_Q_BLOCK, s_)
    nqb = s_ // qb_size
    kb = k_slc.reshape(b_, g_, n_sel, SEL_LEN, d)
    vb = v_slc.reshape(b_, g_, n_sel, SEL_LEN, d)
    gather = jax.vmap(jax.vmap(lambda blocks, ix: blocks[ix]))
    m_tok = n_top * SEL_LEN

    def sel_block(args):
        qq, ix, tt = args
        kg = gather(kb, ix).reshape(b_, g_, qb_size, m_tok, d)
        vg = gather(vb, ix).reshape(b_, g_, qb_size, m_tok, d)
        tok = (ix[..., None] * SEL_LEN + jnp.arange(SEL_LEN)).reshape(b_, g_, 1, qb_size, m_tok)
        mask = tok <= tt[:, None]
        s = jnp.einsum('bgrqd,bgqmd->bgrqm', qq, kg).astype(jnp.float32) * scale
        p = masked_softmax(s, mask)
        return jnp.einsum('bgrqm,bgqmd->bgrqd', p.astype(vg.dtype), vg)

    q_blk = jnp.moveaxis(q.reshape(b_, g_, r_, nqb, qb_size, d), 3, 0)
    ix_blk = jnp.moveaxis(sel_idx.reshape(b_, g_, nqb, qb_size, n_top), 2, 0)
    t_blk = jnp.arange(s_).reshape(nqb, qb_size)
    o_slc = lax.map(sel_block, (q_blk, ix_blk, t_blk))
    o_slc = jnp.moveaxis(o_slc, 0, 3).reshape(b_, g_, r_, s_, d)

    nwb = s_ // WIN_BLOCK
    nback = WINDOW // WIN_BLOCK
    band_len = (nback + 1) * WIN_BLOCK

    def band(a):
        ab = a.reshape(b_, g_, nwb, WIN_BLOCK, d)
        ab = jnp.pad(ab, ((0, 0), (0, 0), (nback, 0), (0, 0), (0, 0)))
        return jnp.concatenate([ab[:, :, i:i + nwb] for i in range(nback + 1)], axis=3)

    rr = np.arange(WIN_BLOCK)[:, None]
    jj = np.arange(band_len)[None, :]
    diff = rr + nback * WIN_BLOCK - jj
    s_abs = (np.arange(nwb)[:, None, None] - nback) * WIN_BLOCK + jj[None]
    mask_w = (diff >= 0) & (diff < WINDOW) & (s_abs >= 0)
    qw = q.reshape(b_, g_, r_, nwb, WIN_BLOCK, d)
    sw = jnp.einsum('bgrcqd,bgckd->bgrcqk', qw, band(k_win)).astype(jnp.float32) * scale
    pw = masked_softmax(sw, mask_w)
    o_win = jnp.einsum('bgrcqk,bgckd->bgrcqd', pw.astype(v_win.dtype), band(v_win)).reshape(b_, g_, r_, s_, d)

    return gates[..., 0:1] * o_cmp + gates[..., 1:2] * o_slc + gates[..., 2:3] * o_win


def hgrn2_chunkwise(q, log_f, k, v):
    b_, h_, s_, dk = q.shape
    L = min(HGRN_CHUNK, s_)
    nc = s_ // L

    def chunk(a):
        return jnp.moveaxis(a.reshape(b_, h_, nc, L, a.shape[-1]), 2, 0)

    causal = np.tril(np.ones((L, L), dtype=bool))[:, :, None]

    def step(state, inp):
        qc, gc, kc, vc = inp
        bc = jnp.cumsum(gc, axis=2)
        o_state = jnp.einsum('bhtd,bhde->bhte', qc * jnp.exp(bc), state)
        rel = jnp.exp(jnp.where(causal, bc[:, :, :, None, :] - bc[:, :, None, :, :], -jnp.inf))
        att = jnp.einsum('bhtd,bhtsd,bhsd->bhts', qc, rel, kc)
        o = o_state + jnp.einsum('bhts,bhse->bhte', att, vc)
        b_last = bc[:, :, -1:]
        state = (jnp.exp(b_last[:, :, 0])[..., None] * state
                 + jnp.einsum('bhsd,bhse->bhde', kc * jnp.exp(b_last - bc), vc))
        return state, o

    init = jnp.zeros((b_, h_, dk, v.shape[-1]), jnp.float32)
    _, os_ = lax.scan(step, init, (chunk(q), chunk(log_f), chunk(k), chunk(v)))
    return jnp.moveaxis(os_, 0, 2).reshape(b_, h_, s_, -1)


def mixer_ab(u, w_in, conv_w, conv_b, gate_b, head_norm, w_out):
    aq, ak, av, ao, ai, af, bq, bk, bv = split_cols(
        u @ w_in, [A_WIDTH] * 4 + [A_HEADS, A_HEADS] + [B_WIDTH] * 3)
    qk = jax.nn.silu(causal_dwconv(jnp.concatenate([aq, ak], axis=-1), conv_w, conv_b))
    aq, ak = jnp.split(qk, 2, axis=-1)
    i_pre = (ai + gate_b[:A_HEADS]).transpose(0, 2, 1)
    f_pre = (af + gate_b[A_HEADS:]).transpose(0, 2, 1)
    h_a = mlstm_chunkwise(split_heads(aq, A_HEADS), split_heads(ak, A_HEADS),
                          split_heads(av, A_HEADS), i_pre, f_pre)
    h_a = merge_heads(head_rms_norm(h_a, head_norm)) * jax.nn.sigmoid(ao.astype(jnp.float32))
    h_b = merge_heads(stick_breaking(split_heads(bq, B_HEADS), split_heads(bk, B_HEADS),
                                     split_heads(bv, B_HEADS)))
    mixed = jnp.concatenate([h_a.astype(u.dtype), h_b.astype(u.dtype)], axis=-1)
    return mixed @ w_out


def mixer_cd(u, w_in, cmp_pos, cmp_w1, cmp_w2, lower_bound, head_norm, w_out):
    b_, s_, _ = u.shape
    (cq, ck_c, cv_c, ck_s, cv_s, ck_w, cv_w, c_gate, dq, df, di, dg) = split_cols(
        u @ w_in, [C_WIDTH] + [KV_WIDTH] * 6 + [3 * C_HEADS] + [D_WIDTH] * 4)
    pos = jnp.arange(s_)
    r_ = C_HEADS // C_KV_HEADS
    q = rope_partial(split_heads(cq, C_HEADS), pos).reshape(b_, C_KV_HEADS, r_, s_, HEAD_DIM)
    gates = jax.nn.sigmoid(c_gate.astype(jnp.float32)).reshape(b_, s_, C_KV_HEADS, r_, 3).transpose(0, 2, 3, 1, 4)
    o_c = nsa_attention(q,
                        split_heads(ck_c, C_KV_HEADS), split_heads(cv_c, C_KV_HEADS),
                        rope_partial(split_heads(ck_s, C_KV_HEADS), pos), split_heads(cv_s, C_KV_HEADS),
                        rope_partial(split_heads(ck_w, C_KV_HEADS), pos), split_heads(cv_w, C_KV_HEADS),
                        gates, cmp_pos, cmp_w1, cmp_w2)
    o_c = merge_heads(o_c.reshape(b_, C_HEADS, s_, HEAD_DIM))
    lb = lower_bound.astype(jnp.float32).reshape(1, D_HEADS, 1, HEAD_DIM)
    fp = split_heads(df, D_HEADS).astype(jnp.float32)
    log_f = jnp.logaddexp(jnp.log(lb), jnp.log1p(-lb) + jax.nn.log_sigmoid(fp))
    k_in = (1.0 - lb) * jax.nn.sigmoid(-fp)
    o_d = hgrn2_chunkwise(split_heads(dq, D_HEADS).astype(jnp.float32), log_f, k_in,
                          split_heads(di, D_HEADS).astype(jnp.float32))
    o_d = merge_heads(head_rms_norm(o_d, head_norm)) * jax.nn.silu(dg.astype(jnp.float32))
    mixed = jnp.concatenate([o_c.astype(u.dtype), o_d.astype(u.dtype)], axis=-1)
    return mixed @ w_out


def conv_ffn(u, w_up, conv_w, conv_b, w_down):
    gate, up = jnp.split(u @ w_up, 2, axis=-1)
    gate = causal_dwconv(gate, conv_w, conv_b)
    return (jax.nn.silu(gate) * up) @ w_down


def setup_inputs(seed: int = 0) -> dict:
    key = jax.random.key(seed)
    ks = iter(jax.random.split(key, 32))
    ne = (DEPTH + 1) // 2
    no = DEPTH // 2

    def nrm(shape, scale):
        return scale * jax.random.normal(next(ks), shape, jnp.float32)

    def gain(shape):
        return 1.0 + nrm(shape, 0.1)

    ab_in = 4 * A_WIDTH + 2 * A_HEADS + 3 * B_WIDTH
    cd_in = C_WIDTH + 6 * KV_WIDTH + 3 * C_HEADS + 4 * D_WIDTH
    x = nrm((BATCH, SEQ, D_MODEL), 1.0)
    norm_mix = gain((DEPTH, D_MODEL))
    norm_ffn = gain((DEPTH, D_MODEL))
    norm_final = gain((D_MODEL,))
    ab_w_in = nrm((ne, D_MODEL, ab_in), D_MODEL ** -0.5)
    ab_conv_w = nrm((ne, A_CONV, 2 * A_WIDTH), A_CONV ** -0.5)
    ab_conv_b = nrm((ne, 2 * A_WIDTH), 0.02)
    ab_gate_b = jnp.concatenate([nrm((ne, A_HEADS), 0.1), 3.0 + nrm((ne, A_HEADS), 0.5)], axis=-1)
    ab_head_norm = gain((ne, A_WIDTH))
    ab_w_out = nrm((ne, A_WIDTH + B_WIDTH, D_MODEL), (A_WIDTH + B_WIDTH) ** -0.5)
    cd_w_in = nrm((no, D_MODEL, cd_in), D_MODEL ** -0.5)
    cd_cmp_pos = nrm((no, 2, CMP_LEN, HEAD_DIM), 0.1)
    cd_cmp_w1 = nrm((no, 2, CMP_LEN, HEAD_DIM, HEAD_DIM), (CMP_LEN * HEAD_DIM) ** -0.5)
    cd_cmp_w2 = nrm((no, 2, HEAD_DIM, HEAD_DIM), HEAD_DIM ** -0.5)
    hgrn_gamma = nrm((DEPTH, D_WIDTH), 0.5)
    cd_head_norm = gain((no, D_WIDTH))
    cd_w_out = nrm((no, C_WIDTH + D_WIDTH, D_MODEL), (C_WIDTH + D_WIDTH) ** -0.5)
    ffn_w_up = nrm((DEPTH, D_MODEL, 2 * FFN_DIM), D_MODEL ** -0.5)
    ffn_conv_w = nrm((DEPTH, FFN_CONV, FFN_DIM), FFN_CONV ** -0.5)
    ffn_conv_b = nrm((DEPTH, FFN_DIM), 0.02)
    ffn_w_down = nrm((DEPTH, FFN_DIM, D_MODEL), FFN_DIM ** -0.5)
    return {'x': x, 'norm_mix': norm_mix, 'norm_ffn': norm_ffn, 'norm_final': norm_final,
            'ab_w_in': ab_w_in, 'ab_conv_w': ab_conv_w, 'ab_conv_b': ab_conv_b, 'ab_gate_b': ab_gate_b,
            'ab_head_norm': ab_head_norm, 'ab_w_out': ab_w_out, 'cd_w_in': cd_w_in,
            'cd_cmp_pos': cd_cmp_pos, 'cd_cmp_w1': cd_cmp_w1, 'cd_cmp_w2': cd_cmp_w2,
            'hgrn_gamma': hgrn_gamma, 'cd_head_norm': cd_head_norm, 'cd_w_out': cd_w_out,
            'ffn_w_up': ffn_w_up, 'ffn_conv_w': ffn_conv_w, 'ffn_conv_b': ffn_conv_b,
            'ffn_w_down': ffn_w_down}


def reference(x, norm_mix, norm_ffn, norm_final, ab_w_in, ab_conv_w, ab_conv_b, ab_gate_b,
              ab_head_norm, ab_w_out, cd_w_in, cd_cmp_pos, cd_cmp_w1, cd_cmp_w2, hgrn_gamma,
              cd_head_norm, cd_w_out, ffn_w_up, ffn_conv_w, ffn_conv_b, ffn_w_down):
    sm = jax.nn.softmax(hgrn_gamma.astype(jnp.float32), axis=0)
    lb_all = jnp.cumsum(sm, axis=0) - sm[0]
    h = x
    for layer in range(DEPTH):
        j = layer // 2
        u = rms_norm(h, norm_mix[layer])
        if layer % 2 == 0:
            h = h + mixer_ab(u, ab_w_in[j], ab_conv_w[j], ab_conv_b[j], ab_gate_b[j],
                             ab_head_norm[j], ab_w_out[j]).astype(h.dtype)
        else:
            h = h + mixer_cd(u, cd_w_in[j], cd_cmp_pos[j], cd_cmp_w1[j], cd_cmp_w2[j],
                             lb_all[layer], cd_head_norm[j], cd_w_out[j]).astype(h.dtype)
        h = h + conv_ffn(rms_norm(h, norm_ffn[layer]), ffn_w_up[layer], ffn_conv_w[layer],
                         ffn_conv_b[layer], ffn_w_down[layer]).astype(h.dtype)
    return rms_norm(h, norm_final)
```

```python
import functools

import numpy as np
import jax
import jax.numpy as jnp
from jax import lax
from jax.experimental import pallas as pl
from jax.experimental.pallas import tpu as pltpu

F32 = jnp.float32
BF16 = jnp.bfloat16

HEAD_DIM = 128
N_HEADS = 8
KV_GROUPS = 2
GROUP_HEADS = N_HEADS // KV_GROUPS
A_CONV = 4
CMP_LEN = 32
CMP_STRIDE = 16
SEL_LEN = 64
SEL_SHIFT = 6
N_SEL = 8
N_LOCAL = 2
FORCE_BONUS = 1000.0
WINDOW = 256
FFN_CONV = 3
ROPE_THETA = 500000.0
ROPE_DIM = HEAD_DIM // 4
RMS_EPS = 1e-6
SCALE = HEAD_DIM ** -0.5

CHUNK = 128
HGRN_LEVELS = (64, 32, 16, 8, 4, 2, 1)
VMEM_LIMIT = 48 * 1024 * 1024


def _cparams(sem):
    return pltpu.CompilerParams(dimension_semantics=sem, vmem_limit_bytes=VMEM_LIMIT)


def _dot(a, b):
    return jnp.dot(a, b, preferred_element_type=F32)


def _dot_nt(a, b):
    return lax.dot_general(a, b, (((1,), (1,)), ((), ())), preferred_element_type=F32)


def _dot_split(a, b01, terms):
    acc = None
    rem = a
    for _ in range(terms):
        piece = rem.astype(BF16)
        part = _dot(piece, b01)
        acc = part if acc is None else acc + part
        rem = rem - piece.astype(F32)
    return acc


def _dot_split_rhs(a01, b, terms):
    acc = None
    rem = b
    for _ in range(terms):
        piece = rem.astype(BF16)
        part = _dot(a01, piece)
        acc = part if acc is None else acc + part
        rem = rem - piece.astype(F32)
    return acc


def _softplus(z):
    return jnp.maximum(z, 0.0) + jnp.log1p(jnp.exp(-jnp.abs(z)))


def _log_sigmoid(z):
    return -_softplus(-z)


def _sigmoid(z):
    return 1.0 / (1.0 + jnp.exp(-z))


def _iota(shape, dim):
    return lax.broadcasted_iota(jnp.int32, shape, dim)


def _rope_swap(x):
    lane = _iota(x.shape, x.ndim - 1)
    up = pltpu.roll(x, HEAD_DIM - ROPE_DIM // 2, axis=x.ndim - 1)
    down = pltpu.roll(x, ROPE_DIM // 2, axis=x.ndim - 1)
    return jnp.where(lane < ROPE_DIM // 2, up, down)


def _rope_tables(pos):
    half = ROPE_DIM // 2
    freqs = ROPE_THETA ** (-np.arange(half, dtype=np.float32) / half)
    ang = jnp.asarray(pos, F32)[:, None] * jnp.asarray(freqs, F32)[None, :]
    cos, sin = jnp.cos(ang), jnp.sin(ang)
    n = ang.shape[0]
    cos_t = jnp.concatenate([cos, cos, jnp.ones((n, HEAD_DIM - ROPE_DIM), F32)], axis=1)
    sin_t = jnp.concatenate([-sin, sin, jnp.zeros((n, HEAD_DIM - ROPE_DIM), F32)], axis=1)
    return cos_t, sin_t


def _norm_proj_kernel(h_ref, g_ref, w_ref, wg_ref, o_ref, og_ref, u_scr):
    @pl.when(pl.program_id(2) == 0)
    def _():
        x = h_ref[...]
        ms = jnp.mean(x * x, axis=-1, keepdims=True)
        u = (x * lax.rsqrt(ms + RMS_EPS) * g_ref[...]).astype(BF16)
        u_scr[...] = u
        og_ref[...] = _dot(u, wg_ref[...])

    res = _dot(u_scr[...], w_ref[...])
    for c in range(o_ref.shape[0]):
        o_ref[c] = res[:, c * HEAD_DIM:(c + 1) * HEAD_DIM].astype(o_ref.dtype)


def _norm_proj(h, g, w, wg, *, tm, tn):
    b, s, d = h.shape
    n = w.shape[1]
    ng = wg.shape[1]
    ncb = tn // HEAD_DIM
    return pl.pallas_call(
        _norm_proj_kernel,
        grid=(b, s // tm, n // tn),
        in_specs=[
            pl.BlockSpec((None, tm, d), lambda bi, si, j: (bi, si, 0)),
            pl.BlockSpec((1, d), lambda bi, si, j: (0, 0)),
            pl.BlockSpec((d, tn), lambda bi, si, j: (0, j)),
            pl.BlockSpec((d, ng), lambda bi, si, j: (0, 0)),
        ],
        out_specs=[
            pl.BlockSpec((None, ncb, tm, HEAD_DIM), lambda bi, si, j: (bi, j, si, 0)),
            pl.BlockSpec((None, tm, ng), lambda bi, si, j: (bi, si, 0)),
        ],
        out_shape=[
            jax.ShapeDtypeStruct((b, n // HEAD_DIM, s, HEAD_DIM), BF16),
            jax.ShapeDtypeStruct((b, s, ng), F32),
        ],
        scratch_shapes=[pltpu.VMEM((tm, d), BF16)],
        compiler_params=_cparams(("parallel", "parallel", "arbitrary")),
        name="norm_proj",
    )(h, g, w, wg)


def _ffn_up_kernel(h_ref, g_ref, wg_ref, wu_ref, cw_ref, cb_ref, o_ref, u_scr, halo_scr):
    si = pl.program_id(1)
    j = pl.program_id(2)
    tm = o_ref.shape[0]

    @pl.when(j == 0)
    def _():
        x = h_ref[...]
        ms = jnp.mean(x * x, axis=-1, keepdims=True)
        u_scr[...] = (x * lax.rsqrt(ms + RMS_EPS) * g_ref[...]).astype(BF16)

    u = u_scr[...]
    gate = _dot(u, wg_ref[...])
    up = _dot(u, wu_ref[...])
    @pl.when(si == 0)
    def _():
        halo_scr[j] = jnp.zeros(halo_scr.shape[1:], F32)

    prev = halo_scr[j]
    halo_scr[j] = gate[tm - 8:, :]
    ext = jnp.concatenate([prev, gate], axis=0)
    cw = cw_ref[...]
    conv = cb_ref[...] + cw[FFN_CONV - 1:FFN_CONV, :] * gate
    for k in range(FFN_CONV - 1):
        off = 8 - (FFN_CONV - 1) + k
        conv = conv + cw[k:k + 1, :] * ext[off:off + tm, :]
    o_ref[...] = (conv * _sigmoid(conv) * up).astype(o_ref.dtype)


def _ffn_up(h, g, wg, wu, cw, cb, *, tm, tn):
    b, s, d = h.shape
    f = wg.shape[1]
    return pl.pallas_call(
        _ffn_up_kernel,
        grid=(b, s // tm, f // tn),
        in_specs=[
            pl.BlockSpec((None, tm, d), lambda bi, si, j: (bi, si, 0)),
            pl.BlockSpec((1, d), lambda bi, si, j: (0, 0)),
            pl.BlockSpec((d, tn), lambda bi, si, j: (0, j)),
            pl.BlockSpec((d, tn), lambda bi, si, j: (0, j)),
            pl.BlockSpec((FFN_CONV, tn), lambda bi, si, j: (0, j)),
            pl.BlockSpec((1, tn), lambda bi, si, j: (0, j)),
        ],
        out_specs=pl.BlockSpec((None, tm, tn), lambda bi, si, j: (bi, si, j)),
        out_shape=jax.ShapeDtypeStruct((b, s, f), BF16),
        scratch_shapes=[pltpu.VMEM((tm, d), BF16), pltpu.VMEM((f // tn, 8, tn), F32)],
        compiler_params=_cparams(("parallel", "arbitrary", "arbitrary")),
        name="ffn_up",
    )(h, g, wg, wu, cw, cb)


def _proj_res_kernel(*refs):
    h_ref, o_ref = refs[0], refs[-1]
    ops = refs[1:-1]
    acc = h_ref[...]
    for i in range(0, len(ops), 2):
        acc = acc + _dot(ops[i][...], ops[i + 1][...])
    o_ref[...] = acc


def _proj_res(h, pairs, *, tm, tn):
    b, s, d = h.shape
    in_specs = [pl.BlockSpec((None, tm, tn), lambda bi, si, j: (bi, si, j))]
    args = [h]
    for a, w in pairs:
        k = a.shape[-1]
        in_specs.append(pl.BlockSpec((None, tm, k), lambda bi, si, j: (bi, si, 0)))
        in_specs.append(pl.BlockSpec((k, tn), lambda bi, si, j: (0, j)))
        args += [a, w]
    return pl.pallas_call(
        _proj_res_kernel,
        grid=(b, s // tm, d // tn),
        in_specs=in_specs,
        out_specs=pl.BlockSpec((None, tm, tn), lambda bi, si, j: (bi, si, j)),
        out_shape=jax.ShapeDtypeStruct((b, s, d), F32),
        compiler_params=_cparams(("parallel", "parallel", "arbitrary")),
        name="proj_res",
    )(*args)


def _rmsnorm_kernel(h_ref, g_ref, o_ref):
    x = h_ref[...]
    ms = jnp.mean(x * x, axis=-1, keepdims=True)
    o_ref[...] = x * lax.rsqrt(ms + RMS_EPS) * g_ref[...]


def _rmsnorm(h, g, *, tm):
    b, s, d = h.shape
    return pl.pallas_call(
        _rmsnorm_kernel,
        grid=(b, s // tm),
        in_specs=[pl.BlockSpec((None, tm, d), lambda bi, si: (bi, si, 0)),
                  pl.BlockSpec((1, d), lambda bi, si: (0, 0))],
        out_specs=pl.BlockSpec((None, tm, d), lambda bi, si: (bi, si, 0)),
        out_shape=jax.ShapeDtypeStruct((b, s, d), F32),
        compiler_params=_cparams(("parallel", "parallel")),
        name="final_norm",
    )(h, g)


def _mlstm_kernel(q_ref, k_ref, v_ref, og_ref, ig_ref, fg_ref, bi_ref, bf_ref,
                  cwq_ref, cwk_ref, cbq_ref, cbk_ref, hn_ref, out_ref,
                  pad_scr, qc_scr, kc_scr, c_scr, b_scr, d_scr):
    s = q_ref.shape[0]
    nch = s // CHUNK

    def conv_silu(src_ref, cw_ref, cb_ref, dst_ref, scale):
        pad_scr[0:8, :] = jnp.zeros((8, HEAD_DIM), F32)
        for p in range(nch):
            pad_scr[8 + p * CHUNK:8 + (p + 1) * CHUNK, :] = src_ref[p * CHUNK:(p + 1) * CHUNK, :].astype(F32)
        cw = cw_ref[...]
        for p in range(nch):
            acc = cb_ref[...] + cw[0:1, :] * pad_scr[pl.ds(8 - (A_CONV - 1) + p * CHUNK, CHUNK), :]
            for j in range(1, A_CONV):
                acc = acc + cw[j:j + 1, :] * pad_scr[pl.ds(8 - (A_CONV - 1) + j + p * CHUNK, CHUNK), :]
            y = acc * _sigmoid(acc)
            if scale != 1.0:
                y = y * scale
            dst_ref[p * CHUNK:(p + 1) * CHUNK, :] = y.astype(BF16)

    conv_silu(q_ref, cwq_ref, cbq_ref, qc_scr, 1.0)
    conv_silu(k_ref, cwk_ref, cbk_ref, kc_scr, SCALE)

    log_f = _log_sigmoid(fg_ref[...] + bf_ref[...])
    log_i = ig_ref[...] + bi_ref[...]
    r_i = _iota((CHUNK, CHUNK), 0)
    c_i = _iota((CHUNK, CHUNK), 1)
    upper = (r_i <= c_i).astype(BF16)
    b_all = _dot_split(log_f, upper, 3)
    b_scr[...] = b_all
    d_scr[...] = log_i - b_all
    c_scr[...] = jnp.zeros(c_scr.shape, F32)
    causal = c_i <= r_i
    ones_blk = jnp.ones((CHUNK, HEAD_DIM), BF16)
    hn = hn_ref[...]

    def body(c, m):
        t0 = pl.multiple_of(c * CHUNK, CHUNK)
        q = qc_scr[pl.ds(t0, CHUNK), :]
        k = kc_scr[pl.ds(t0, CHUNK), :]
        v_aug = jnp.concatenate([v_ref[pl.ds(t0, CHUNK), :], ones_blk], axis=1)
        b_row = b_scr[pl.ds(c, 1), :]
        d_row = d_scr[pl.ds(c, 1), :]
        b_col = jnp.transpose(jnp.broadcast_to(b_row, (CHUNK, CHUNK)))
        d_col = jnp.transpose(jnp.broadcast_to(d_row, (CHUNK, CHUNK)))
        dmat = jnp.where(causal, b_col + d_row, -jnp.inf)
        g = b_col + m
        m_row = jnp.maximum(g, jnp.max(dmat, axis=1, keepdims=True))
        w = jnp.exp(dmat - m_row) * _dot_nt(q, k)
        w_state = jnp.exp(g - m_row)
        inter = _dot(q, c_scr[...].astype(BF16))
        intra = _dot(w.astype(BF16), v_aug)
        tot = jnp.concatenate([w_state, w_state], axis=1) * inter + intra
        num = tot[:, :HEAD_DIM]
        den = tot[:, HEAD_DIM:]
        hh = num / jnp.maximum(jnp.abs(den), jnp.exp(-m_row))
        y = hh * lax.rsqrt(jnp.mean(hh * hh, axis=-1, keepdims=True) + RMS_EPS) * hn
        y = y * _sigmoid(og_ref[pl.ds(t0, CHUNK), :].astype(F32))
        out_ref[pl.ds(t0, CHUNK), :] = y.astype(out_ref.dtype)
        b_last = b_col[CHUNK - 1:CHUNK, :]
        w_new = b_last + d_col
        m_new = jnp.maximum(b_last + m, jnp.max(w_new, axis=0, keepdims=True))
        decay = jnp.exp(b_last + m - m_new)
        kw = jnp.exp(w_new - m_new) * k.astype(F32)
        kw_t = jnp.transpose(kw).astype(BF16)
        c_scr[...] = jnp.concatenate([decay, decay], axis=1) * c_scr[...] + _dot(kw_t, v_aug)
        return m_new

    lax.fori_loop(0, nch, body, jnp.zeros((1, CHUNK), F32))


def _mlstm(p, gi, gf, bias_i, bias_f, cw, cb, hn):
    b, _, s, _ = p.shape
    nch = s // CHUNK
    slab = lambda off: pl.BlockSpec((None, None, s, HEAD_DIM), lambda bi, hi: (bi, off + hi, 0, 0))
    gate = pl.BlockSpec((None, None, nch, CHUNK), lambda bi, hi: (bi, hi, 0, 0))
    hb = pl.BlockSpec((None, 1, CHUNK), lambda bi, hi: (hi, 0, 0))
    return pl.pallas_call(
        _mlstm_kernel,
        grid=(b, N_HEADS),
        in_specs=[
            slab(0), slab(N_HEADS), slab(2 * N_HEADS), slab(3 * N_HEADS), gate, gate, hb, hb,
            pl.BlockSpec((A_CONV, HEAD_DIM), lambda bi, hi: (0, hi)),
            pl.BlockSpec((A_CONV, HEAD_DIM), lambda bi, hi: (0, N_HEADS + hi)),
            pl.BlockSpec((1, HEAD_DIM), lambda bi, hi: (0, hi)),
            pl.BlockSpec((1, HEAD_DIM), lambda bi, hi: (0, N_HEADS + hi)),
            pl.BlockSpec((1, HEAD_DIM), lambda bi, hi: (0, hi)),
        ],
        out_specs=pl.BlockSpec((None, s, HEAD_DIM), lambda bi, hi: (bi, 0, hi)),
        out_shape=jax.ShapeDtypeStruct((b, s, N_HEADS * HEAD_DIM), BF16),
        scratch_shapes=[
            pltpu.VMEM((s + 8, HEAD_DIM), F32),
            pltpu.VMEM((s, HEAD_DIM), BF16),
            pltpu.VMEM((s, HEAD_DIM), BF16),
            pltpu.VMEM((HEAD_DIM, 2 * HEAD_DIM), F32),
            pltpu.VMEM((nch, CHUNK), F32),
            pltpu.VMEM((nch, CHUNK), F32),
        ],
        compiler_params=_cparams(("parallel", "parallel")),
        name="mlstm",
    )(p, p, p, p, gi, gf, bias_i, bias_f, cw, cw, cb, cb, hn)


def _sb_kernel(q_ref, k_ref, v_ref, o_ref):
    qi = pl.program_id(2)
    tq = q_ref.shape[0]
    q = q_ref[...]
    r_i = _iota((tq, tq), 0)
    c_i = _iota((tq, tq), 1)
    suffix = (r_i > c_i).astype(BF16)
    strict = c_i < r_i

    def block(kb, carry, diag):
        acc, run = carry
        k0 = pl.multiple_of(kb * tq, tq)
        k = k_ref[pl.ds(k0, tq), :]
        v = v_ref[pl.ds(k0, tq), :]
        z = _dot_nt(q, k) * SCALE
        sp = _softplus(z)
        log_keep = -sp
        if diag:
            log_keep = jnp.where(strict, log_keep, 0.0)
        later = _dot_split(log_keep, suffix, 2) + run
        a = jnp.exp(z - sp + later)
        if diag:
            a = jnp.where(strict, a, 0.0)
        acc = acc + _dot(a.astype(BF16), v)
        run = run + jnp.sum(log_keep, axis=1, keepdims=True)
        return acc, run

    carry = (jnp.zeros((tq, HEAD_DIM), F32), jnp.zeros((tq, 1), F32))
    carry = block(qi, carry, True)
    acc, _ = lax.fori_loop(0, qi, lambda it, cr: block(qi - 1 - it, cr, False), carry)
    o_ref[...] = acc.astype(o_ref.dtype)


def _stick_breaking(p, *, q_off, k_off, v_off, tq):
    b, _, s, _ = p.shape
    return pl.pallas_call(
        _sb_kernel,
        grid=(b, N_HEADS, s // tq),
        in_specs=[
            pl.BlockSpec((None, None, tq, HEAD_DIM), lambda bi, hi, qi: (bi, q_off + hi, qi, 0)),
            pl.BlockSpec((None, None, s, HEAD_DIM), lambda bi, hi, qi: (bi, k_off + hi, 0, 0)),
            pl.BlockSpec((None, None, s, HEAD_DIM), lambda bi, hi, qi: (bi, v_off + hi, 0, 0)),
        ],
        out_specs=pl.BlockSpec((None, tq, HEAD_DIM), lambda bi, hi, qi: (bi, qi, hi)),
        out_shape=jax.ShapeDtypeStruct((b, s, N_HEADS * HEAD_DIM), BF16),
        compiler_params=_cparams(("parallel", "parallel", "arbitrary")),
        name="stick_breaking",
    )(p, p, p)


def _gelu_tanh(x):
    return 0.5 * x * (1.0 + jnp.tanh(np.sqrt(2.0 / np.pi).astype(np.float32) * (x + 0.044715 * (x * x * x))))


def _nsa_prep_kernel(kcg_ref, vcg_ref, ks_ref, kw_ref, pos_ref, w1_ref, w2_ref,
                     cos_ref, sin_ref, cosc_ref, sinc_ref,
                     kc_out, vc_out, ks_out, kw_out):
    def compress(src_ref, idx):
        grp = src_ref[...].astype(F32)
        xa = _dot((grp + pos_ref[idx, 0:1, :]).astype(BF16), w1_ref[idx, 0])
        xb = _dot((grp + pos_ref[idx, 1:2, :]).astype(BF16), w1_ref[idx, 1])
        n = xb.shape[0]
        hid = _gelu_tanh(xa + pltpu.roll(xb, n - 1, axis=0))
        return _dot(hid.astype(BF16), w2_ref[idx])

    kc = compress(kcg_ref, 0)
    kc_out[...] = (kc * cosc_ref[...] + _rope_swap(kc) * sinc_ref[...]).astype(kc_out.dtype)
    vc_out[...] = compress(vcg_ref, 1).astype(vc_out.dtype)
    for src, dst in ((ks_ref, ks_out), (kw_ref, kw_out)):
        x = src[...].astype(F32)
        dst[...] = (x * cos_ref[...] + _rope_swap(x) * sin_ref[...]).astype(dst.dtype)


def _nsa_prep(p, pos, w1, w2, cos, sin, cosc, sinc, *, kc_off, vc_off, ks_off, kw_off):
    b, nslab, s, _ = p.shape
    ng = s // CMP_STRIDE
    pg = p.reshape(b, nslab, ng, CMP_STRIDE * HEAD_DIM)
    grp = lambda off: pl.BlockSpec((None, None, ng, CMP_STRIDE * HEAD_DIM), lambda bi, gi: (bi, off + gi, 0, 0))
    slab = lambda off: pl.BlockSpec((None, None, s, HEAD_DIM), lambda bi, gi: (bi, off + gi, 0, 0))
    full = lambda a: pl.BlockSpec(a.shape, lambda bi, gi: (0,) * a.ndim)
    out_c = pl.BlockSpec((None, None, ng, HEAD_DIM), lambda bi, gi: (bi, gi, 0, 0))
    out_s = pl.BlockSpec((None, None, s, HEAD_DIM), lambda bi, gi: (bi, gi, 0, 0))
    return pl.pallas_call(
        _nsa_prep_kernel,
        grid=(b, KV_GROUPS),
        in_specs=[grp(kc_off), grp(vc_off), slab(ks_off), slab(kw_off),
                  full(pos), full(w1), full(w2), full(cos), full(sin), full(cosc), full(sinc)],
        out_specs=[out_c, out_c, out_s, out_s],
        out_shape=[jax.ShapeDtypeStruct((b, KV_GROUPS, ng, HEAD_DIM), BF16),
                   jax.ShapeDtypeStruct((b, KV_GROUPS, ng, HEAD_DIM), BF16),
                   jax.ShapeDtypeStruct((b, KV_GROUPS, s, HEAD_DIM), BF16),
                   jax.ShapeDtypeStruct((b, KV_GROUPS, s, HEAD_DIM), BF16)],
        compiler_params=_cparams(("parallel", "parallel")),
        name="nsa_prep",
    )(pg, pg, p, p, pos, w1, w2, cos, sin, cosc, sinc)


def _masked_softmax(sc, mask):
    sc = jnp.where(mask, sc, -jnp.inf)
    m = jnp.max(sc, axis=-1, keepdims=True)
    m = jnp.where(m > -jnp.inf, m, 0.0)
    e = jnp.exp(sc - m)
    return e / jnp.maximum(jnp.sum(e, axis=-1, keepdims=True), 1e-30)


def _nsa_kernel(q_ref, kc_ref, vc_ref, ks_ref, vs_ref, kw_ref, vw_ref, gate_ref, cos_ref, sin_ref,
                cover_ref, o_ref, *, tk):
    qi = pl.program_id(2)
    r = q_ref.shape[0]
    tq = q_ref.shape[1]
    s_len = ks_ref.shape[0]
    n_cmp = (s_len - CMP_LEN) // CMP_STRIDE + 1
    t0 = qi * tq

    qf = q_ref[...].astype(F32).reshape(r * tq, HEAD_DIM)
    cos = jnp.concatenate([cos_ref[...]] * r, axis=0)
    sin = jnp.concatenate([sin_ref[...]] * r, axis=0)
    qb = (qf * cos + _rope_swap(qf) * sin).astype(BF16)

    t_col = t0 + _iota((tq, HEAD_DIM), 0)
    lane = _iota((tq, HEAD_DIM), 1)

    ncl = kc_ref.shape[0]
    sc = (_dot_nt(qb, kc_ref[...]) * SCALE).reshape(r, tq, ncl)
    cmp_ok = ((lane * CMP_STRIDE + (CMP_LEN - 1) <= t_col) & (lane < n_cmp))[:, :ncl]
    p_cmp = _masked_softmax(sc, cmp_ok[None])
    o_cmp = _dot(p_cmp.reshape(r * tq, ncl).astype(BF16), vc_ref[...])

    p_sum = p_cmp[0]
    for h in range(1, r):
        p_sum = p_sum + p_cmp[h]
    p_slc = _dot_split(p_sum, cover_ref[...], 3)
    blk_t = t_col >> SEL_SHIFT
    started = lane <= blk_t
    forced = (lane == 0) | (started & (blk_t - lane < N_LOCAL))
    score = jnp.where(started, p_slc + jnp.where(forced, FORCE_BONUS, 0.0), -jnp.inf)
    rank = jnp.zeros((tq, HEAD_DIM), F32)
    for jp in range(s_len // SEL_LEN):
        col = score[:, jp:jp + 1]
        beats = (col > score) | ((col == score) & (lane > jp))
        rank = rank + jnp.where(beats, 1.0, 0.0)
    sel = jnp.where((rank < N_SEL) & started, 1.0, 0.0).astype(BF16)

    e_row = _iota((HEAD_DIM, tk), 0)
    e_blk = _iota((HEAD_DIM, tk), 1) >> SEL_SHIFT
    s_loc = _iota((tq, tk), 1)
    t_q = t0 + _iota((tq, tk), 0)

    def sel_tile(kt, carry):
        m, l, acc = carry
        k0 = pl.multiple_of(kt * tk, tk)
        k = ks_ref[pl.ds(k0, tk), :]
        v = vs_ref[pl.ds(k0, tk), :]
        expand = (e_row == e_blk + kt * (tk // SEL_LEN)).astype(BF16)
        ok = (_dot(sel, expand) > 0.5) & (k0 + s_loc <= t_q)
        sc_t = (_dot_nt(qb, k) * SCALE).reshape(r, tq, tk)
        sc_t = jnp.where(ok[None], sc_t, -jnp.inf)
        m_new = jnp.maximum(m, jnp.max(sc_t, axis=-1, keepdims=True))
        alpha = jnp.exp(m - m_new)
        e = jnp.exp(sc_t - m_new)
        l = alpha * l + jnp.sum(e, axis=-1, keepdims=True)
        pv = _dot(e.reshape(r * tq, tk).astype(BF16), v).reshape(r, tq, HEAD_DIM)
        return m_new, l, alpha * acc + pv

    n_tiles = (t0 + tq + tk - 1) // tk
    init = (jnp.full((r, tq, 1), -jnp.inf, F32), jnp.zeros((r, tq, 1), F32),
            jnp.zeros((r, tq, HEAD_DIM), F32))
    _, l_s, acc_s = lax.fori_loop(0, n_tiles, sel_tile, init)
    o_slc = acc_s / jnp.maximum(l_s, 1e-30)

    band = WINDOW + tq
    w0 = pl.multiple_of(jnp.maximum(t0 - WINDOW, 0), tq)
    kw = kw_ref[pl.ds(w0, band), :]
    vw = vw_ref[pl.ds(w0, band), :]
    diff = (t0 + _iota((tq, band), 0)) - (w0 + _iota((tq, band), 1))
    win_ok = (diff >= 0) & (diff < WINDOW)
    sw = (_dot_nt(qb, kw) * SCALE).reshape(r, tq, band)
    p_win = _masked_softmax(sw, win_ok[None])
    o_win = _dot(p_win.reshape(r * tq, band).astype(BF16), vw).reshape(r, tq, HEAD_DIM)

    o_cmp = o_cmp.reshape(r, tq, HEAD_DIM)
    gates = _sigmoid(gate_ref[...])
    for h in range(r):
        out = (gates[:, 3 * h:3 * h + 1] * o_cmp[h] + gates[:, 3 * h + 1:3 * h + 2] * o_slc[h]
               + gates[:, 3 * h + 2:3 * h + 3] * o_win[h])
        o_ref[:, h * HEAD_DIM:(h + 1) * HEAD_DIM] = out.astype(o_ref.dtype)


def _nsa(p, kc, vc, ks, kw, gates, cos, sin, cover, *, vs_off, vw_off, tq, tk):
    b, _, s, _ = p.shape
    r = GROUP_HEADS
    ng = kc.shape[2]
    kv = lambda: pl.BlockSpec((None, None, s, HEAD_DIM), lambda bi, gi, qi: (bi, gi, 0, 0))
    pslab = lambda off: pl.BlockSpec((None, None, s, HEAD_DIM), lambda bi, gi, qi: (bi, off + gi, 0, 0))
    cmp_spec = pl.BlockSpec((None, None, ng, HEAD_DIM), lambda bi, gi, qi: (bi, gi, 0, 0))
    return pl.pallas_call(
        functools.partial(_nsa_kernel, tk=tk),
        grid=(b, KV_GROUPS, s // tq),
        in_specs=[
            pl.BlockSpec((None, r, tq, HEAD_DIM), lambda bi, gi, qi: (bi, gi, qi, 0)),
            cmp_spec, cmp_spec, kv(), pslab(vs_off), kv(), pslab(vw_off),
            pl.BlockSpec((None, tq, HEAD_DIM), lambda bi, gi, qi: (bi, qi, gi)),
            pl.BlockSpec((tq, HEAD_DIM), lambda bi, gi, qi: (qi, 0)),
            pl.BlockSpec((tq, HEAD_DIM), lambda bi, gi, qi: (qi, 0)),
            pl.BlockSpec(cover.shape, lambda bi, gi, qi: (0, 0)),
        ],
        out_specs=pl.BlockSpec((None, tq, r * HEAD_DIM), lambda bi, gi, qi: (bi, qi, gi)),
        out_shape=jax.ShapeDtypeStruct((b, s, N_HEADS * HEAD_DIM), BF16),
        compiler_params=_cparams(("parallel", "parallel", "arbitrary")),
        name="nsa",
    )(p, kc, vc, ks, p, kw, p, gates, cos, sin, cover)


def _hgrn_tables():
    r = np.arange(CHUNK)[:, None]
    j = np.arange(CHUNK)[None, :]
    mats = []
    for sz in HGRN_LEVELS:
        mid = (r // (2 * sz)) * (2 * sz) + sz
        right = (r // sz) % 2 == 1
        mats.append(np.where(right, (j >= mid) & (j <= r), (j > r) & (j <= mid - 1)))
    mats.append(j <= r)
    mats.append(j > r)
    return np.concatenate(mats, axis=0).astype(np.float32)


def _hgrn_kernel(q_ref, f_ref, i_ref, g_ref, lb_ref, hn_ref, tab_ref, out_ref, st_scr):
    s = q_ref.shape[0]
    nch = s // CHUNK
    nlev = len(HGRN_LEVELS)
    lb = lb_ref[...]
    log_lb = jnp.log(lb)
    log_1m = jnp.log1p(-lb)
    hn = hn_ref[...]
    r_i = _iota((CHUNK, CHUNK), 0)
    c_i = _iota((CHUNK, CHUNK), 1)
    tab = tab_ref[...]
    st_scr[...] = jnp.zeros(st_scr.shape, F32)

    def body(c, carry):
        t0 = pl.multiple_of(c * CHUNK, CHUNK)
        q = q_ref[pl.ds(t0, CHUNK), :].astype(F32)
        fp = f_ref[pl.ds(t0, CHUNK), :].astype(F32)
        v = i_ref[pl.ds(t0, CHUNK), :]
        a2 = log_1m - _softplus(-fp)
        hi = jnp.maximum(log_lb, a2)
        lo = jnp.minimum(log_lb, a2)
        log_f = hi + jnp.log1p(jnp.exp(lo - hi))
        k = (1.0 - lb) * _sigmoid(fp * -1.0)
        x_all = _dot_split_rhs(tab, log_f, 2)
        att = jnp.where(r_i == c_i, jnp.sum(q * k, axis=1, keepdims=True), 0.0)
        for lv, sz in enumerate(HGRN_LEVELS):
            e = jnp.exp(x_all[lv * CHUNK:(lv + 1) * CHUNK, :])
            sh = sz.bit_length() - 1
            right = ((_iota((CHUNK, HEAD_DIM), 0) >> sh) & 1) == 1
            qt = jnp.where(right, q * e, 0.0).astype(BF16)
            kt = jnp.where(right, 0.0, k * e).astype(BF16)
            same = (r_i >> (sh + 1)) == (c_i >> (sh + 1))
            att = att + jnp.where(same, _dot_nt(qt, kt), 0.0)
        bc = x_all[nlev * CHUNK:(nlev + 1) * CHUNK, :]
        rev = x_all[(nlev + 1) * CHUNK:(nlev + 2) * CHUNK, :]
        st = st_scr[...]
        o = _dot(att.astype(BF16), v) + _dot_nt((q * jnp.exp(bc)).astype(BF16), st.astype(BF16))
        y = o * lax.rsqrt(jnp.mean(o * o, axis=-1, keepdims=True) + RMS_EPS) * hn
        gg = g_ref[pl.ds(t0, CHUNK), :].astype(F32)
        out_ref[pl.ds(t0, CHUNK), :] = (y * (gg * _sigmoid(gg))).astype(out_ref.dtype)
        v_t = jnp.transpose(v.astype(F32)).astype(BF16)
        st_scr[...] = st * jnp.exp(bc[CHUNK - 1:CHUNK, :]) + _dot(v_t, (k * jnp.exp(rev)).astype(BF16))
        return carry

    lax.fori_loop(0, nch, body, 0)


def _hgrn(p, lb, hn, tab, *, q_off, f_off, i_off, g_off):
    b, _, s, _ = p.shape
    slab = lambda off: pl.BlockSpec((None, None, s, HEAD_DIM), lambda bi, hi: (bi, off + hi, 0, 0))
    return pl.pallas_call(
        _hgrn_kernel,
        grid=(b, N_HEADS),
        in_specs=[slab(q_off), slab(f_off), slab(i_off), slab(g_off),
                  pl.BlockSpec((1, HEAD_DIM), lambda bi, hi: (0, hi)),
                  pl.BlockSpec((1, HEAD_DIM), lambda bi, hi: (0, hi)),
                  pl.BlockSpec(tab.shape, lambda bi, hi: (0, 0))],
        out_specs=pl.BlockSpec((None, s, HEAD_DIM), lambda bi, hi: (bi, 0, hi)),
        out_shape=jax.ShapeDtypeStruct((b, s, N_HEADS * HEAD_DIM), BF16),
        scratch_shapes=[pltpu.VMEM((HEAD_DIM, HEAD_DIM), F32)],
        compiler_params=_cparams(("parallel", "parallel")),
        name="hgrn2",
    )(p, p, p, p, lb, hn, tab)


def _pad_cols(w, n):
    return jnp.pad(w, ((0, 0), (0, n - w.shape[1])))


def _tiles(s):
    tm = min(512, s)
    return tm


def _layer_ab(h, norm_g, w_in, conv_w, conv_b, gate_b, head_norm, w_out):
    b, s, d = h.shape
    aw = N_HEADS * HEAD_DIM
    tm = _tiles(s)
    w_main = jnp.concatenate([w_in[:, :4 * aw], w_in[:, 4 * aw + 2 * N_HEADS:]], axis=1).astype(BF16)
    w_gate = _pad_cols(w_in[:, 4 * aw:4 * aw + 2 * N_HEADS], HEAD_DIM).astype(BF16)
    p, gates = _norm_proj(h, norm_g.reshape(1, d), w_main, w_gate, tm=tm, tn=512)
    nch = s // CHUNK
    g_t = jnp.transpose(gates[:, :, :2 * N_HEADS], (0, 2, 1))
    gi = g_t[:, :N_HEADS].reshape(b, N_HEADS, nch, CHUNK)
    gf = g_t[:, N_HEADS:].reshape(b, N_HEADS, nch, CHUNK)
    bias_i = jnp.broadcast_to(gate_b[:N_HEADS, None, None], (N_HEADS, 1, CHUNK)).astype(F32)
    bias_f = jnp.broadcast_to(gate_b[N_HEADS:, None, None], (N_HEADS, 1, CHUNK)).astype(F32)
    h_a = _mlstm(p, gi, gf, bias_i, bias_f, conv_w, conv_b.reshape(1, -1), head_norm.reshape(1, -1))
    h_b = _stick_breaking(p, q_off=4 * N_HEADS, k_off=5 * N_HEADS, v_off=6 * N_HEADS, tq=min(256, s))
    w_o = w_out.astype(BF16)
    return _proj_res(h, [(h_a, w_o[:aw]), (h_b, w_o[aw:])], tm=tm, tn=512)


def _layer_cd(h, norm_g, w_in, cmp_pos, cmp_w1, cmp_w2, lower_bound, head_norm, w_out):
    b, s, d = h.shape
    cw = N_HEADS * HEAD_DIM
    kvw = KV_GROUPS * HEAD_DIM
    tm = _tiles(s)
    n_gate = 3 * N_HEADS
    g0 = cw + 6 * kvw
    w_main = jnp.concatenate([w_in[:, :g0], w_in[:, g0 + n_gate:]], axis=1).astype(BF16)
    per_group = 3 * GROUP_HEADS
    w_gate = jnp.concatenate(
        [_pad_cols(w_in[:, g0 + g * per_group:g0 + (g + 1) * per_group], HEAD_DIM) for g in range(KV_GROUPS)],
        axis=1).astype(BF16)
    p, gates = _norm_proj(h, norm_g.reshape(1, d), w_main, w_gate, tm=tm, tn=512)
    half = CMP_LEN // CMP_STRIDE
    pos = cmp_pos.reshape(2, half, CMP_STRIDE * HEAD_DIM).astype(F32)
    w1 = cmp_w1.reshape(2, half, CMP_STRIDE * HEAD_DIM, HEAD_DIM).astype(BF16)
    w2 = cmp_w2.astype(BF16)
    cos, sin = _rope_tables(np.arange(s))
    ng = s // CMP_STRIDE
    cosc, sinc = _rope_tables(np.arange(ng) * CMP_STRIDE + CMP_LEN - 1)
    kc, vc, ks, kw = _nsa_prep(p, pos, w1, w2, cos, sin, cosc, sinc,
                               kc_off=8, vc_off=10, ks_off=12, kw_off=16)
    n_cmp = (s - CMP_LEN) // CMP_STRIDE + 1
    n_sel = s // SEL_LEN
    cs = np.arange(n_cmp) * CMP_STRIDE
    ss = np.arange(n_sel) * SEL_LEN
    cover = np.zeros((ng, HEAD_DIM), np.float32)
    cover[:n_cmp, :n_sel] = (cs[:, None] < ss[None, :] + SEL_LEN) & (cs[:, None] + CMP_LEN > ss[None, :])
    o_c = _nsa(p, kc, vc, ks, kw, gates, cos, sin, jnp.asarray(cover, BF16),
               vs_off=14, vw_off=18, tq=128, tk=min(256, s))
    o_d = _hgrn(p, lower_bound.reshape(1, -1).astype(F32), head_norm.reshape(1, -1),
                jnp.asarray(_hgrn_tables(), BF16), q_off=20, f_off=28, i_off=36, g_off=44)
    w_o = w_out.astype(BF16)
    return _proj_res(h, [(o_c, w_o[:cw]), (o_d, w_o[cw:])], tm=tm, tn=512)


def _ffn(h, norm_g, w_up, conv_w, conv_b, w_down):
    b, s, d = h.shape
    f = w_down.shape[0]
    tm = _tiles(s)
    act = _ffn_up(h, norm_g.reshape(1, d), w_up[:, :f].astype(BF16), w_up[:, f:].astype(BF16),
                  conv_w, conv_b.reshape(1, f), tm=tm, tn=512)
    return _proj_res(h, [(act, w_down.astype(BF16))], tm=tm, tn=512)


def kernel(x, norm_mix, norm_ffn, norm_final, ab_w_in, ab_conv_w, ab_conv_b, ab_gate_b, ab_head_norm, ab_w_out, cd_w_in, cd_cmp_pos, cd_cmp_w1, cd_cmp_w2, hgrn_gamma, cd_head_norm, cd_w_out, ffn_w_up, ffn_conv_w, ffn_conv_b, ffn_w_down):
    depth = norm_mix.shape[0]
    sm = jax.nn.softmax(hgrn_gamma.astype(F32), axis=0)
    lb_all = jnp.cumsum(sm, axis=0) - sm[0]
    h = x
    for layer in range(depth):
        j = layer // 2
        if layer % 2 == 0:
            h = _layer_ab(h, norm_mix[layer], ab_w_in[j], ab_conv_w[j], ab_conv_b[j], ab_gate_b[j],
                          ab_head_norm[j], ab_w_out[j])
        else:
            h = _layer_cd(h, norm_mix[layer], cd_w_in[j], cd_cmp_pos[j], cd_cmp_w1[j], cd_cmp_w2[j],
                          lb_all[layer], cd_head_norm[j], cd_w_out[j])
        h = _ffn(h, norm_ffn[layer], ffn_w_up[layer], ffn_conv_w[layer], ffn_conv_b[layer], ffn_w_down[layer])
    return _rmsnorm(h, norm_final.reshape(1, -1), tm=_tiles(h.shape[1]))
```

```python
import functools

import numpy as np
import jax
import jax.numpy as jnp
from jax import lax
from jax.experimental import pallas as pl
from jax.experimental.pallas import tpu as pltpu

F32 = jnp.float32
BF16 = jnp.bfloat16

HEAD_DIM = 128
N_HEADS = 8
KV_GROUPS = 2
GROUP_HEADS = N_HEADS // KV_GROUPS
A_CONV = 4
CMP_LEN = 32
CMP_STRIDE = 16
SEL_LEN = 64
SEL_SHIFT = 6
N_SEL = 8
N_LOCAL = 2
FORCE_BONUS = 1000.0
WINDOW = 256
FFN_CONV = 3
ROPE_THETA = 500000.0
ROPE_DIM = HEAD_DIM // 4
RMS_EPS = 1e-6
SCALE = HEAD_DIM ** -0.5

CHUNK = 128
HEADS_PER_STEP = 2
HGRN_LEVELS = (64, 32, 16, 8, 4, 2, 1)
MXU_COLS = 256
VMEM_LIMIT = 56 * 1024 * 1024


def _cparams(sem):
    return pltpu.CompilerParams(dimension_semantics=sem, vmem_limit_bytes=VMEM_LIMIT)


def _dot(a, b):
    return jnp.dot(a, b, preferred_element_type=F32)


def _dot_nt(a, b):
    return lax.dot_general(a, b, (((1,), (1,)), ((), ())), preferred_element_type=F32)


def _dot_split(a, b01, terms):
    acc = None
    rem = a
    for _ in range(terms):
        piece = rem.astype(BF16)
        part = _dot(piece, b01)
        acc = part if acc is None else acc + part
        rem = rem - piece.astype(F32)
    return acc


def _dot_split_rhs(a01, b, terms):
    acc = None
    rem = b
    for _ in range(terms):
        piece = rem.astype(BF16)
        part = _dot(a01, piece)
        acc = part if acc is None else acc + part
        rem = rem - piece.astype(F32)
    return acc


def _dot_nt_split(a01, b, terms):
    acc = None
    rem = b
    for _ in range(terms):
        piece = rem.astype(BF16)
        part = _dot_nt(a01, piece)
        acc = part if acc is None else acc + part
        rem = rem - piece.astype(F32)
    return acc


def _softplus(z):
    return jnp.maximum(z, 0.0) + jnp.log(1.0 + jnp.exp(-jnp.abs(z)))


def _log_sigmoid(z):
    return -_softplus(-z)


def _sigmoid(z):
    return 1.0 / (1.0 + jnp.exp(-z))


def _iota(shape, dim):
    return lax.broadcasted_iota(jnp.int32, shape, dim)


def _rope_swap(x):
    lane = _iota(x.shape, x.ndim - 1)
    up = pltpu.roll(x, HEAD_DIM - ROPE_DIM // 2, axis=x.ndim - 1)
    down = pltpu.roll(x, ROPE_DIM // 2, axis=x.ndim - 1)
    return jnp.where(lane < ROPE_DIM // 2, up, down)


def _rope_tables(pos):
    half = ROPE_DIM // 2
    freqs = ROPE_THETA ** (-np.arange(half, dtype=np.float32) / half)
    ang = jnp.asarray(pos, F32)[:, None] * jnp.asarray(freqs, F32)[None, :]
    cos, sin = jnp.cos(ang), jnp.sin(ang)
    n = ang.shape[0]
    cos_t = jnp.concatenate([cos, cos, jnp.ones((n, HEAD_DIM - ROPE_DIM), F32)], axis=1)
    sin_t = jnp.concatenate([-sin, sin, jnp.zeros((n, HEAD_DIM - ROPE_DIM), F32)], axis=1)
    return cos_t, sin_t


def _rms_to_bf16(h_ref, g_ref):
    x = h_ref[...]
    ms = jnp.mean(x * x, axis=-1, keepdims=True)
    return (x * lax.rsqrt(ms + RMS_EPS) * g_ref[...]).astype(BF16)


def _norm_proj_kernel(h_ref, g_ref, w_ref, wg_ref, o_ref, og_ref, u_scr):
    @pl.when(pl.program_id(2) == 0)
    def _():
        u = _rms_to_bf16(h_ref, g_ref)
        u_scr[...] = u
        og_ref[...] = _dot(u, wg_ref[...])

    u = u_scr[...]
    tn = w_ref.shape[1]
    for c0 in range(0, tn, MXU_COLS):
        res = _dot(u, w_ref[:, c0:c0 + MXU_COLS])
        for c in range(MXU_COLS // HEAD_DIM):
            o_ref[c0 // HEAD_DIM + c] = res[:, c * HEAD_DIM:(c + 1) * HEAD_DIM].astype(o_ref.dtype)


def _norm_proj(h, g, w, wg, *, tm, tn):
    b, s, d = h.shape
    n = w.shape[1]
    ng = wg.shape[1]
    ncb = tn // HEAD_DIM
    return pl.pallas_call(
        _norm_proj_kernel,
        grid=(b, s // tm, n // tn),
        in_specs=[
            pl.BlockSpec((None, tm, d), lambda bi, si, j: (bi, si, 0)),
            pl.BlockSpec((1, d), lambda bi, si, j: (0, 0)),
            pl.BlockSpec((d, tn), lambda bi, si, j: (0, j)),
            pl.BlockSpec((d, ng), lambda bi, si, j: (0, 0)),
        ],
        out_specs=[
            pl.BlockSpec((None, ncb, tm, HEAD_DIM), lambda bi, si, j: (bi, j, si, 0)),
            pl.BlockSpec((None, tm, ng), lambda bi, si, j: (bi, si, 0)),
        ],
        out_shape=[
            jax.ShapeDtypeStruct((b, n // HEAD_DIM, s, HEAD_DIM), BF16),
            jax.ShapeDtypeStruct((b, s, ng), F32),
        ],
        scratch_shapes=[pltpu.VMEM((tm, d), BF16)],
        compiler_params=_cparams(("parallel", "parallel", "arbitrary")),
        name="norm_proj",
    )(h, g, w, wg)


def _ffn_up_kernel(h_ref, g_ref, wg_ref, wu_ref, cw_ref, cb_ref, o_ref, u_scr, halo_scr):
    si = pl.program_id(1)
    j = pl.program_id(2)
    tm, tn = o_ref.shape

    @pl.when(j == 0)
    def _():
        u_scr[...] = _rms_to_bf16(h_ref, g_ref)

    @pl.when(si == 0)
    def _():
        halo_scr[j] = jnp.zeros(halo_scr.shape[1:], F32)

    u = u_scr[...]
    cw = cw_ref[...]
    cb = cb_ref[...]
    for c0 in range(0, tn, MXU_COLS):
        cols = slice(c0, c0 + MXU_COLS)
        gate = _dot(u, wg_ref[:, cols])
        up = _dot(u, wu_ref[:, cols])
        ext = jnp.concatenate([halo_scr[j, :, cols], gate], axis=0)
        halo_scr[j, :, cols] = gate[tm - 8:, :]
        conv = cb[:, cols] + cw[FFN_CONV - 1:FFN_CONV, cols] * gate
        for k in range(FFN_CONV - 1):
            off = 8 - (FFN_CONV - 1) + k
            conv = conv + cw[k:k + 1, cols] * ext[off:off + tm, :]
        o_ref[:, cols] = (conv * _sigmoid(conv) * up).astype(o_ref.dtype)


def _ffn_up(h, g, wg, wu, cw, cb, *, tm, tn):
    b, s, d = h.shape
    f = wg.shape[1]
    return pl.pallas_call(
        _ffn_up_kernel,
        grid=(b, s // tm, f // tn),
        in_specs=[
            pl.BlockSpec((None, tm, d), lambda bi, si, j: (bi, si, 0)),
            pl.BlockSpec((1, d), lambda bi, si, j: (0, 0)),
            pl.BlockSpec((d, tn), lambda bi, si, j: (0, j)),
            pl.BlockSpec((d, tn), lambda bi, si, j: (0, j)),
            pl.BlockSpec((FFN_CONV, tn), lambda bi, si, j: (0, j)),
            pl.BlockSpec((1, tn), lambda bi, si, j: (0, j)),
        ],
        out_specs=pl.BlockSpec((None, tm, tn), lambda bi, si, j: (bi, si, j)),
        out_shape=jax.ShapeDtypeStruct((b, s, f), BF16),
        scratch_shapes=[pltpu.VMEM((tm, d), BF16), pltpu.VMEM((f // tn, 8, tn), F32)],
        compiler_params=_cparams(("parallel", "arbitrary", "arbitrary")),
        name="ffn_up",
    )(h, g, wg, wu, cw, cb)


def _proj_res_kernel(*refs):
    h_ref, o_ref = refs[0], refs[-1]
    ops = refs[1:-1]
    acc = h_ref[...]
    for i in range(0, len(ops), 2):
        acc = acc + _dot(ops[i][...], ops[i + 1][...])
    o_ref[...] = acc


def _proj_res(h, pairs, *, tm, tn):
    b, s, d = h.shape
    in_specs = [pl.BlockSpec((None, tm, tn), lambda bi, si, j: (bi, si, j))]
    args = [h]
    for a, w in pairs:
        k = a.shape[-1]
        in_specs.append(pl.BlockSpec((None, tm, k), lambda bi, si, j: (bi, si, 0)))
        in_specs.append(pl.BlockSpec((k, tn), lambda bi, si, j: (0, j)))
        args += [a, w]
    return pl.pallas_call(
        _proj_res_kernel,
        grid=(b, s // tm, d // tn),
        in_specs=in_specs,
        out_specs=pl.BlockSpec((None, tm, tn), lambda bi, si, j: (bi, si, j)),
        out_shape=jax.ShapeDtypeStruct((b, s, d), F32),
        compiler_params=_cparams(("parallel", "parallel", "arbitrary")),
        name="proj_res",
    )(*args)


def _rmsnorm_kernel(h_ref, g_ref, o_ref):
    x = h_ref[...]
    ms = jnp.mean(x * x, axis=-1, keepdims=True)
    o_ref[...] = x * lax.rsqrt(ms + RMS_EPS) * g_ref[...]


def _rmsnorm(h, g, *, tm):
    b, s, d = h.shape
    return pl.pallas_call(
        _rmsnorm_kernel,
        grid=(b, s // tm),
        in_specs=[pl.BlockSpec((None, tm, d), lambda bi, si: (bi, si, 0)),
                  pl.BlockSpec((1, d), lambda bi, si: (0, 0))],
        out_specs=pl.BlockSpec((None, tm, d), lambda bi, si: (bi, si, 0)),
        out_shape=jax.ShapeDtypeStruct((b, s, d), F32),
        compiler_params=_cparams(("parallel", "parallel")),
        name="final_norm",
    )(h, g)


def _mlstm_kernel(q_ref, k_ref, v_ref, og_ref, ig_ref, fg_ref, bi_ref, bf_ref,
                  cwq_ref, cwk_ref, cbq_ref, cbk_ref, hn_ref, out_ref,
                  pad_scr, qc_scr, kc_scr, c_scr, b_scr, d_scr):
    nh, s, _ = q_ref.shape
    nch = s // CHUNK

    def conv_silu(src_ref, cw_ref, cb_ref, dst_ref, hh, scale):
        lanes = slice(hh * HEAD_DIM, (hh + 1) * HEAD_DIM)
        pad_scr[0:8, :] = jnp.zeros((8, HEAD_DIM), F32)
        for p in range(nch):
            pad_scr[8 + p * CHUNK:8 + (p + 1) * CHUNK, :] = src_ref[hh, p * CHUNK:(p + 1) * CHUNK, :].astype(F32)
        cw = cw_ref[:, lanes]
        for p in range(nch):
            acc = cb_ref[:, lanes] + cw[0:1, :] * pad_scr[pl.ds(8 - (A_CONV - 1) + p * CHUNK, CHUNK), :]
            for j in range(1, A_CONV):
                acc = acc + cw[j:j + 1, :] * pad_scr[pl.ds(8 - (A_CONV - 1) + j + p * CHUNK, CHUNK), :]
            y = acc * _sigmoid(acc)
            if scale != 1.0:
                y = y * scale
            dst_ref[hh, p * CHUNK:(p + 1) * CHUNK, :] = y.astype(BF16)

    r_i = _iota((CHUNK, CHUNK), 0)
    c_i = _iota((CHUNK, CHUNK), 1)
    upper = (r_i <= c_i).astype(BF16)
    for hh in range(nh):
        conv_silu(q_ref, cwq_ref, cbq_ref, qc_scr, hh, 1.0)
        conv_silu(k_ref, cwk_ref, cbk_ref, kc_scr, hh, SCALE)
        log_f = _log_sigmoid(fg_ref[hh] + bf_ref[hh])
        log_i = ig_ref[hh] + bi_ref[hh]
        b_all = _dot_split(log_f, upper, 3)
        b_scr[hh] = b_all
        d_scr[hh] = log_i - b_all
    c_scr[...] = jnp.zeros(c_scr.shape, F32)
    causal = c_i <= r_i
    ones_blk = jnp.ones((CHUNK, HEAD_DIM), BF16)

    def head_chunk(hh, c, t0, m):
        lanes = slice(hh * HEAD_DIM, (hh + 1) * HEAD_DIM)
        q = qc_scr[hh, pl.ds(t0, CHUNK), :]
        k = kc_scr[hh, pl.ds(t0, CHUNK), :]
        v_aug = jnp.concatenate([v_ref[hh, pl.ds(t0, CHUNK), :], ones_blk], axis=1)
        b_row = b_scr[hh, pl.ds(c, 1), :]
        d_row = d_scr[hh, pl.ds(c, 1), :]
        b_col = jnp.transpose(jnp.broadcast_to(b_row, (CHUNK, CHUNK)))
        d_col = jnp.transpose(jnp.broadcast_to(d_row, (CHUNK, CHUNK)))
        dmat = jnp.where(causal, b_col + d_row, -jnp.inf)
        g = b_col + m
        m_row = jnp.maximum(g, jnp.max(dmat, axis=1, keepdims=True))
        w = jnp.exp(dmat - m_row) * _dot_nt(q, k)
        w_state = jnp.exp(g - m_row)
        c_old = c_scr[hh]
        inter = _dot(q, c_old.astype(BF16))
        intra = _dot(w.astype(BF16), v_aug)
        tot = jnp.concatenate([w_state, w_state], axis=1) * inter + intra
        num = tot[:, :HEAD_DIM]
        den = tot[:, HEAD_DIM:]
        hv = num / jnp.maximum(jnp.abs(den), jnp.exp(-m_row))
        y = hv * lax.rsqrt(jnp.mean(hv * hv, axis=-1, keepdims=True) + RMS_EPS) * hn_ref[:, lanes]
        y = y * _sigmoid(og_ref[hh, pl.ds(t0, CHUNK), :].astype(F32))
        out_ref[pl.ds(t0, CHUNK), lanes] = y.astype(out_ref.dtype)
        b_last = b_col[CHUNK - 1:CHUNK, :]
        w_new = b_last + d_col
        m_new = jnp.maximum(b_last + m, jnp.max(w_new, axis=0, keepdims=True))
        decay = jnp.exp(b_last + m - m_new)
        kw = jnp.exp(w_new - m_new) * k.astype(F32)
        kw_t = jnp.transpose(kw).astype(BF16)
        c_scr[hh] = jnp.concatenate([decay, decay], axis=1) * c_old + _dot(kw_t, v_aug)
        return m_new

    def body(c, ms):
        t0 = pl.multiple_of(c * CHUNK, CHUNK)
        return tuple(head_chunk(hh, c, t0, ms[hh]) for hh in range(nh))

    lax.fori_loop(0, nch, body, tuple(jnp.zeros((1, CHUNK), F32) for _ in range(nh)))


def _mlstm(p, gi, gf, bias_i, bias_f, cw, cb, hn):
    b, _, s, _ = p.shape
    nch = s // CHUNK
    nh = HEADS_PER_STEP
    nblk = N_HEADS // nh
    w = nh * HEAD_DIM
    slab = lambda off: pl.BlockSpec((None, nh, s, HEAD_DIM), lambda bi, hi: (bi, off + hi, 0, 0))
    gate = pl.BlockSpec((None, nh, nch, CHUNK), lambda bi, hi: (bi, hi, 0, 0))
    hb = pl.BlockSpec((nh, 1, CHUNK), lambda bi, hi: (hi, 0, 0))
    return pl.pallas_call(
        _mlstm_kernel,
        grid=(b, nblk),
        in_specs=[
            slab(0), slab(nblk), slab(2 * nblk), slab(3 * nblk), gate, gate, hb, hb,
            pl.BlockSpec((A_CONV, w), lambda bi, hi: (0, hi)),
            pl.BlockSpec((A_CONV, w), lambda bi, hi: (0, nblk + hi)),
            pl.BlockSpec((1, w), lambda bi, hi: (0, hi)),
            pl.BlockSpec((1, w), lambda bi, hi: (0, nblk + hi)),
            pl.BlockSpec((1, w), lambda bi, hi: (0, hi)),
        ],
        out_specs=pl.BlockSpec((None, s, w), lambda bi, hi: (bi, 0, hi)),
        out_shape=jax.ShapeDtypeStruct((b, s, N_HEADS * HEAD_DIM), BF16),
        scratch_shapes=[
            pltpu.VMEM((s + 8, HEAD_DIM), F32),
            pltpu.VMEM((nh, s, HEAD_DIM), BF16),
            pltpu.VMEM((nh, s, HEAD_DIM), BF16),
            pltpu.VMEM((nh, HEAD_DIM, 2 * HEAD_DIM), F32),
            pltpu.VMEM((nh, nch, CHUNK), F32),
            pltpu.VMEM((nh, nch, CHUNK), F32),
        ],
        compiler_params=_cparams(("parallel", "parallel")),
        name="mlstm",
    )(p, p, p, p, gi, gf, bias_i, bias_f, cw, cw, cb, cb, hn)


def _sb_kernel(q_ref, k_ref, v_ref, o_ref):
    qi = pl.program_id(2)
    sub = q_ref.shape[0] // 2
    r_i = _iota((sub, sub), 0)
    c_i = _iota((sub, sub), 1)
    suffix = (r_i > c_i).astype(BF16)
    strict = c_i < r_i
    q_lo = q_ref[0:sub, :]
    q_hi = q_ref[sub:2 * sub, :]

    def sub_block(q, k, v, run, diag):
        z = _dot_nt(q, k)
        sp = _softplus(z)
        log_keep = -sp
        if diag:
            log_keep = jnp.where(strict, log_keep, 0.0)
        later = _dot(log_keep.astype(BF16), suffix) + run
        a = jnp.exp(z - sp + later)
        if diag:
            a = jnp.where(strict, a, 0.0)
        return _dot(a.astype(BF16), v), jnp.sum(log_keep, axis=1, keepdims=True)

    def tile(kt, carry, diag):
        acc_lo, run_lo, acc_hi, run_hi = carry
        k0 = pl.multiple_of(kt * 2 * sub, 2 * sub)
        k_a = k_ref[pl.ds(k0, sub), :]
        v_a = v_ref[pl.ds(k0, sub), :]
        k_b = k_ref[pl.ds(k0 + sub, sub), :]
        v_b = v_ref[pl.ds(k0 + sub, sub), :]
        o_hb, s_hb = sub_block(q_hi, k_b, v_b, run_hi, diag)
        o_ha, s_ha = sub_block(q_hi, k_a, v_a, run_hi + s_hb, False)
        if diag:
            o_la, s_la = sub_block(q_lo, k_a, v_a, run_lo, True)
            acc_lo = acc_lo + o_la
            run_lo = run_lo + s_la
        else:
            o_lb, s_lb = sub_block(q_lo, k_b, v_b, run_lo, False)
            o_la, s_la = sub_block(q_lo, k_a, v_a, run_lo + s_lb, False)
            acc_lo = acc_lo + (o_lb + o_la)
            run_lo = run_lo + (s_lb + s_la)
        return acc_lo, run_lo, acc_hi + (o_hb + o_ha), run_hi + (s_hb + s_ha)

    zero_acc = jnp.zeros((sub, HEAD_DIM), F32)
    zero_run = jnp.zeros((sub, 1), F32)
    carry = tile(qi, (zero_acc, zero_run, zero_acc, zero_run), True)
    acc_lo, _, acc_hi, _ = lax.fori_loop(0, qi, lambda it, cr: tile(qi - 1 - it, cr, False), carry)
    o_ref[0:sub, :] = acc_lo.astype(o_ref.dtype)
    o_ref[sub:2 * sub, :] = acc_hi.astype(o_ref.dtype)


def _stick_breaking(p, *, q_off, k_off, v_off, tq):
    b, _, s, _ = p.shape
    return pl.pallas_call(
        _sb_kernel,
        grid=(b, N_HEADS, s // tq),
        in_specs=[
            pl.BlockSpec((None, None, tq, HEAD_DIM), lambda bi, hi, qi: (bi, q_off + hi, qi, 0)),
            pl.BlockSpec((None, None, s, HEAD_DIM), lambda bi, hi, qi: (bi, k_off + hi, 0, 0)),
            pl.BlockSpec((None, None, s, HEAD_DIM), lambda bi, hi, qi: (bi, v_off + hi, 0, 0)),
        ],
        out_specs=pl.BlockSpec((None, tq, HEAD_DIM), lambda bi, hi, qi: (bi, qi, hi)),
        out_shape=jax.ShapeDtypeStruct((b, s, N_HEADS * HEAD_DIM), BF16),
        compiler_params=_cparams(("parallel", "parallel", "arbitrary")),
        name="stick_breaking",
    )(p, p, p)


def _gelu_tanh(x):
    return 0.5 * x * (1.0 + jnp.tanh(np.sqrt(2.0 / np.pi).astype(np.float32) * (x + 0.044715 * (x * x * x))))


def _nsa_prep_kernel(kcs_ref, vcs_ref, ks_ref, kw_ref, pos_ref, w1_ref, w2_ref,
                     cos_ref, sin_ref, cosc_ref, sinc_ref,
                     kc_out, vc_out, ks_out, kw_out, f32_scr):
    s = ks_ref.shape[0]
    ng = s // CMP_STRIDE

    def compress(src_ref, idx):
        f32_scr[...] = src_ref[...].astype(F32)
        xa = jnp.zeros((ng, HEAD_DIM), F32)
        xb = jnp.zeros((ng, HEAD_DIM), F32)
        for l in range(CMP_STRIDE):
            rows = f32_scr[pl.ds(l, ng, stride=CMP_STRIDE), :]
            xa = xa + _dot((rows + pos_ref[idx, l:l + 1, :]).astype(BF16), w1_ref[idx, l])
            xb = xb + _dot((rows + pos_ref[idx, CMP_STRIDE + l:CMP_STRIDE + l + 1, :]).astype(BF16),
                           w1_ref[idx, CMP_STRIDE + l])
        hid = _gelu_tanh(xa + pltpu.roll(xb, ng - 1, axis=0))
        return _dot(hid.astype(BF16), w2_ref[idx])

    kc = compress(kcs_ref, 0)
    kc_out[...] = (kc * cosc_ref[...] + _rope_swap(kc) * sinc_ref[...]).astype(kc_out.dtype)
    vc_out[...] = compress(vcs_ref, 1).astype(vc_out.dtype)
    for src, dst in ((ks_ref, ks_out), (kw_ref, kw_out)):
        x = src[...].astype(F32)
        dst[...] = (x * cos_ref[...] + _rope_swap(x) * sin_ref[...]).astype(dst.dtype)


def _nsa_prep(p, pos, w1, w2, cos, sin, cosc, sinc, *, kc_off, vc_off, ks_off, kw_off):
    b, _, s, _ = p.shape
    ng = s // CMP_STRIDE
    slab = lambda off: pl.BlockSpec((None, None, s, HEAD_DIM), lambda bi, gi: (bi, off + gi, 0, 0))
    full = lambda a: pl.BlockSpec(a.shape, lambda bi, gi: (0,) * a.ndim)
    out_c = pl.BlockSpec((None, None, ng, HEAD_DIM), lambda bi, gi: (bi, gi, 0, 0))
    out_s = pl.BlockSpec((None, None, s, HEAD_DIM), lambda bi, gi: (bi, gi, 0, 0))
    return pl.pallas_call(
        _nsa_prep_kernel,
        grid=(b, KV_GROUPS),
        in_specs=[slab(kc_off), slab(vc_off), slab(ks_off), slab(kw_off),
                  full(pos), full(w1), full(w2), full(cos), full(sin), full(cosc), full(sinc)],
        out_specs=[out_c, out_c, out_s, out_s],
        out_shape=[jax.ShapeDtypeStruct((b, KV_GROUPS, ng, HEAD_DIM), BF16),
                   jax.ShapeDtypeStruct((b, KV_GROUPS, ng, HEAD_DIM), BF16),
                   jax.ShapeDtypeStruct((b, KV_GROUPS, s, HEAD_DIM), BF16),
                   jax.ShapeDtypeStruct((b, KV_GROUPS, s, HEAD_DIM), BF16)],
        scratch_shapes=[pltpu.VMEM((s, HEAD_DIM), F32)],
        compiler_params=_cparams(("parallel", "parallel")),
        name="nsa_prep",
    )(p, p, p, p, pos, w1, w2, cos, sin, cosc, sinc)


def _masked_softmax(sc, mask):
    sc = jnp.where(mask, sc, -jnp.inf)
    m = jnp.max(sc, axis=-1, keepdims=True)
    m = jnp.where(m > -jnp.inf, m, 0.0)
    e = jnp.exp(sc - m)
    return e / jnp.maximum(jnp.sum(e, axis=-1, keepdims=True), 1e-30)


def _nsa_kernel(q_ref, kc_ref, vc_ref, ks_ref, vs_ref, kw_ref, vw_ref, gate_ref, cos_ref, sin_ref,
                cover_t_ref, o_ref, *, tk):
    qi = pl.program_id(2)
    r = q_ref.shape[0]
    tq = q_ref.shape[1]
    s_len = ks_ref.shape[0]
    n_cmp = (s_len - CMP_LEN) // CMP_STRIDE + 1
    t0 = qi * tq

    qf = q_ref[...].astype(F32).reshape(r * tq, HEAD_DIM)
    cos = jnp.concatenate([cos_ref[...]] * r, axis=0)
    sin = jnp.concatenate([sin_ref[...]] * r, axis=0)
    qb = (qf * cos + _rope_swap(qf) * sin).astype(BF16)

    t_col = t0 + _iota((tq, HEAD_DIM), 0)
    lane = _iota((tq, HEAD_DIM), 1)

    ncl = kc_ref.shape[0]
    sc = _dot_nt(qb, kc_ref[...]).reshape(r, tq, ncl)
    cmp_ok = ((lane * CMP_STRIDE + (CMP_LEN - 1) <= t_col) & (lane < n_cmp))[:, :ncl]
    p_cmp = _masked_softmax(sc, cmp_ok[None])
    o_cmp = _dot(p_cmp.reshape(r * tq, ncl).astype(BF16), vc_ref[...]).reshape(r, tq, HEAD_DIM)

    band = WINDOW + tq
    w0 = pl.multiple_of(jnp.maximum(t0 - WINDOW, 0), tq)
    k_win = kw_ref[pl.ds(w0, band), :]
    v_win = vw_ref[pl.ds(w0, band), :]
    diff = (t0 + _iota((tq, band), 0)) - (w0 + _iota((tq, band), 1))
    win_ok = (diff >= 0) & (diff < WINDOW)
    sw = _dot_nt(qb, k_win).reshape(r, tq, band)
    p_win = _masked_softmax(sw, win_ok[None])
    o_win = _dot(p_win.reshape(r * tq, band).astype(BF16), v_win).reshape(r, tq, HEAD_DIM)

    p_sum = p_cmp[0]
    for h in range(1, r):
        p_sum = p_sum + p_cmp[h]
    n_blk = s_len // SEL_LEN
    p_slc_t = _dot_nt_split(cover_t_ref[...], p_sum, 3)[:n_blk, :]
    blk = _iota((n_blk, tq), 0)
    blk_t = (t0 + _iota((n_blk, tq), 1)) >> SEL_SHIFT
    started = blk <= blk_t
    forced = (blk == 0) | (started & (blk_t - blk < N_LOCAL))
    score = jnp.where(started, p_slc_t + jnp.where(forced, FORCE_BONUS, 0.0), -jnp.inf)
    rank = jnp.zeros((n_blk, tq), F32)
    for jp in range(n_blk):
        row = score[jp:jp + 1, :]
        beats = (row > score) | ((row == score) & (blk > jp))
        rank = rank + jnp.where(beats, 1.0, 0.0)
    sel_t = jnp.where((rank < N_SEL) & started, 1.0, 0.0)
    sel_t = jnp.concatenate([sel_t, jnp.zeros((HEAD_DIM - n_blk, tq), F32)], axis=0)
    sel = jnp.transpose(sel_t).astype(BF16)

    e_row = _iota((HEAD_DIM, tk), 0)
    e_blk = _iota((HEAD_DIM, tk), 1) >> SEL_SHIFT
    s_loc = _iota((tq, tk), 1)
    t_q = t0 + _iota((tq, tk), 0)
    n_chain = 2
    hc = r // n_chain
    q_chain = [qb[c * hc * tq:(c + 1) * hc * tq, :] for c in range(n_chain)]
    ones_blk = jnp.ones((tk, HEAD_DIM), BF16)

    def sel_tile(kt, carry):
        k0 = pl.multiple_of(kt * tk, tk)
        k = ks_ref[pl.ds(k0, tk), :]
        v_aug = jnp.concatenate([vs_ref[pl.ds(k0, tk), :], ones_blk], axis=1)
        expand = (e_row == e_blk + kt * (tk // SEL_LEN)).astype(BF16)
        ok = (_dot(sel, expand) > 0.5) & (k0 + s_loc <= t_q)
        bias = jnp.where(ok, 0.0, -jnp.inf)[None]
        new = []
        for c in range(n_chain):
            m, acc = carry[c]
            sc_t = _dot_nt(q_chain[c], k).reshape(hc, tq, tk) + bias
            m_new = jnp.maximum(m, jnp.max(sc_t, axis=-1, keepdims=True))
            alpha = jnp.exp(m - m_new)
            e = jnp.exp(sc_t - m_new)
            pv = _dot(e.reshape(hc * tq, tk).astype(BF16), v_aug).reshape(hc, tq, 2 * HEAD_DIM)
            new.append((m_new, alpha * acc + pv))
        return tuple(new)

    n_tiles = (t0 + tq + tk - 1) // tk
    init = tuple((jnp.full((hc, tq, 1), -jnp.inf, F32), jnp.zeros((hc, tq, 2 * HEAD_DIM), F32))
                 for _ in range(n_chain))
    chains = lax.fori_loop(0, n_tiles, sel_tile, init)

    gates = _sigmoid(gate_ref[...])
    for h in range(r):
        acc_s = chains[h // hc][1][h % hc]
        o_slc = acc_s[:, :HEAD_DIM] / jnp.maximum(acc_s[:, HEAD_DIM:], 1e-30)
        out = (gates[:, 3 * h:3 * h + 1] * o_cmp[h] + gates[:, 3 * h + 1:3 * h + 2] * o_slc
               + gates[:, 3 * h + 2:3 * h + 3] * o_win[h])
        o_ref[:, h * HEAD_DIM:(h + 1) * HEAD_DIM] = out.astype(o_ref.dtype)


def _nsa(p, kc, vc, ks, kw, gates, cos, sin, cover, *, vs_off, vw_off, tq, tk):
    b, _, s, _ = p.shape
    r = GROUP_HEADS
    ng = kc.shape[2]
    kv = lambda: pl.BlockSpec((None, None, s, HEAD_DIM), lambda bi, gi, qi: (bi, gi, 0, 0))
    pslab = lambda off: pl.BlockSpec((None, None, s, HEAD_DIM), lambda bi, gi, qi: (bi, off + gi, 0, 0))
    cmp_spec = pl.BlockSpec((None, None, ng, HEAD_DIM), lambda bi, gi, qi: (bi, gi, 0, 0))
    return pl.pallas_call(
        functools.partial(_nsa_kernel, tk=tk),
        grid=(b, KV_GROUPS, s // tq),
        in_specs=[
            pl.BlockSpec((None, r, tq, HEAD_DIM), lambda bi, gi, qi: (bi, gi, qi, 0)),
            cmp_spec, cmp_spec, kv(), pslab(vs_off), kv(), pslab(vw_off),
            pl.BlockSpec((None, tq, HEAD_DIM), lambda bi, gi, qi: (bi, qi, gi)),
            pl.BlockSpec((tq, HEAD_DIM), lambda bi, gi, qi: (qi, 0)),
            pl.BlockSpec((tq, HEAD_DIM), lambda bi, gi, qi: (qi, 0)),
            pl.BlockSpec(cover.shape, lambda bi, gi, qi: (0, 0)),
        ],
        out_specs=pl.BlockSpec((None, tq, r * HEAD_DIM), lambda bi, gi, qi: (bi, qi, gi)),
        out_shape=jax.ShapeDtypeStruct((b, s, N_HEADS * HEAD_DIM), BF16),
        compiler_params=_cparams(("parallel", "parallel", "arbitrary")),
        name="nsa",
    )(p, kc, vc, ks, p, kw, p, gates, cos, sin, cover)


def _hgrn_tables():
    r = np.arange(CHUNK)[:, None]
    j = np.arange(CHUNK)[None, :]
    mats = []
    for sz in HGRN_LEVELS:
        mid = (r // (2 * sz)) * (2 * sz) + sz
        right = (r // sz) % 2 == 1
        mats.append(np.where(right, (j >= mid) & (j <= r), (j > r) & (j <= mid - 1)))
    mats.append(j <= r)
    mats.append(j > r)
    return np.concatenate(mats, axis=0).astype(np.float32)


def _hgrn_kernel(q_ref, f_ref, i_ref, g_ref, lb_ref, hn_ref, tab_ref, out_ref, st_scr):
    nh, s, _ = q_ref.shape
    nch = s // CHUNK
    nlev = len(HGRN_LEVELS)
    r_i = _iota((CHUNK, CHUNK), 0)
    c_i = _iota((CHUNK, CHUNK), 1)
    row = _iota((CHUNK, HEAD_DIM), 0)
    st_scr[...] = jnp.zeros(st_scr.shape, F32)

    def head_chunk(hh, t0):
        lanes = slice(hh * HEAD_DIM, (hh + 1) * HEAD_DIM)
        lb = lb_ref[:, lanes]
        q = q_ref[hh, pl.ds(t0, CHUNK), :].astype(F32)
        fp = f_ref[hh, pl.ds(t0, CHUNK), :].astype(F32)
        v = i_ref[hh, pl.ds(t0, CHUNK), :]
        log_lb = jnp.log(lb)
        a2 = jnp.log1p(-lb) - _softplus(-fp)
        hi = jnp.maximum(log_lb, a2)
        lo = jnp.minimum(log_lb, a2)
        log_f = hi + jnp.log(1.0 + jnp.exp(lo - hi))
        k = (1.0 - lb) * _sigmoid(-fp)
        x_all = _dot_split_rhs(tab_ref[...], log_f, 2)
        att = jnp.where(r_i == c_i, jnp.sum(q * k, axis=1, keepdims=True), 0.0)
        for lv, sz in enumerate(HGRN_LEVELS):
            sh = sz.bit_length() - 1
            e = jnp.exp(x_all[lv * CHUNK:(lv + 1) * CHUNK, :])
            right = ((row >> sh) & 1) == 1
            qt = jnp.where(right, q * e, 0.0).astype(BF16)
            kt = jnp.where(right, 0.0, k * e).astype(BF16)
            same = (r_i >> (sh + 1)) == (c_i >> (sh + 1))
            att = att + jnp.where(same, _dot_nt(qt, kt), 0.0)
        bc = x_all[nlev * CHUNK:(nlev + 1) * CHUNK, :]
        rev = x_all[(nlev + 1) * CHUNK:(nlev + 2) * CHUNK, :]
        st = st_scr[hh]
        o = _dot(att.astype(BF16), v) + _dot_nt((q * jnp.exp(bc)).astype(BF16), st.astype(BF16))
        y = o * lax.rsqrt(jnp.mean(o * o, axis=-1, keepdims=True) + RMS_EPS) * hn_ref[:, lanes]
        gg = g_ref[hh, pl.ds(t0, CHUNK), :].astype(F32)
        out_ref[pl.ds(t0, CHUNK), lanes] = (y * (gg * _sigmoid(gg))).astype(out_ref.dtype)
        v_t = jnp.transpose(v.astype(F32)).astype(BF16)
        st_scr[hh] = st * jnp.exp(bc[CHUNK - 1:CHUNK, :]) + _dot(v_t, (k * jnp.exp(rev)).astype(BF16))

    def body(c, carry):
        t0 = pl.multiple_of(c * CHUNK, CHUNK)
        for hh in range(nh):
            head_chunk(hh, t0)
        return carry

    lax.fori_loop(0, nch, body, 0)


def _hgrn(p, lb, hn, tab, *, q_off, f_off, i_off, g_off):
    b, _, s, _ = p.shape
    nh = HEADS_PER_STEP
    nblk = N_HEADS // nh
    w = nh * HEAD_DIM
    slab = lambda off: pl.BlockSpec((None, nh, s, HEAD_DIM), lambda bi, hi: (bi, off // nh + hi, 0, 0))
    return pl.pallas_call(
        _hgrn_kernel,
        grid=(b, nblk),
        in_specs=[slab(q_off), slab(f_off), slab(i_off), slab(g_off),
                  pl.BlockSpec((1, w), lambda bi, hi: (0, hi)),
                  pl.BlockSpec((1, w), lambda bi, hi: (0, hi)),
                  pl.BlockSpec(tab.shape, lambda bi, hi: (0, 0))],
        out_specs=pl.BlockSpec((None, s, w), lambda bi, hi: (bi, 0, hi)),
        out_shape=jax.ShapeDtypeStruct((b, s, N_HEADS * HEAD_DIM), BF16),
        scratch_shapes=[pltpu.VMEM((nh, HEAD_DIM, HEAD_DIM), F32)],
        compiler_params=_cparams(("parallel", "parallel")),
        name="hgrn2",
    )(p, p, p, p, lb, hn, tab)


def _pad_cols(w, n):
    return jnp.pad(w, ((0, 0), (0, n - w.shape[1])))


def _row_tile(s):
    return min(1024, s)


def _layer_ab(h, norm_g, w_in, conv_w, conv_b, gate_b, head_norm, w_out):
    b, s, d = h.shape
    aw = N_HEADS * HEAD_DIM
    tm = _row_tile(s)
    g0 = 4 * aw
    g1 = g0 + 2 * N_HEADS
    w_main = jnp.concatenate([w_in[:, :g0], w_in[:, g1:g1 + aw] * SCALE, w_in[:, g1 + aw:]], axis=1).astype(BF16)
    w_gate = _pad_cols(w_in[:, g0:g1], HEAD_DIM).astype(BF16)
    p, gates = _norm_proj(h, norm_g.reshape(1, d), w_main, w_gate, tm=tm, tn=512)
    nch = s // CHUNK
    g_t = jnp.transpose(gates[:, :, :2 * N_HEADS], (0, 2, 1))
    gi = g_t[:, :N_HEADS].reshape(b, N_HEADS, nch, CHUNK)
    gf = g_t[:, N_HEADS:].reshape(b, N_HEADS, nch, CHUNK)
    bias_i = jnp.broadcast_to(gate_b[:N_HEADS, None, None], (N_HEADS, 1, CHUNK)).astype(F32)
    bias_f = jnp.broadcast_to(gate_b[N_HEADS:, None, None], (N_HEADS, 1, CHUNK)).astype(F32)
    h_a = _mlstm(p, gi, gf, bias_i, bias_f, conv_w, conv_b.reshape(1, -1), head_norm.reshape(1, -1))
    h_b = _stick_breaking(p, q_off=4 * N_HEADS, k_off=5 * N_HEADS, v_off=6 * N_HEADS, tq=min(512, s))
    w_o = w_out.astype(BF16)
    return _proj_res(h, [(h_a, w_o[:aw]), (h_b, w_o[aw:])], tm=tm, tn=512)


def _layer_cd(h, norm_g, w_in, cmp_pos, cmp_w1, cmp_w2, lower_bound, head_norm, w_out):
    b, s, d = h.shape
    cw = N_HEADS * HEAD_DIM
    kvw = KV_GROUPS * HEAD_DIM
    tm = _row_tile(s)
    n_gate = 3 * N_HEADS
    g0 = cw + 6 * kvw
    w_main = jnp.concatenate([w_in[:, :cw] * SCALE, w_in[:, cw:g0], w_in[:, g0 + n_gate:]], axis=1).astype(BF16)
    per_group = 3 * GROUP_HEADS
    w_gate = jnp.concatenate(
        [_pad_cols(w_in[:, g0 + g * per_group:g0 + (g + 1) * per_group], HEAD_DIM) for g in range(KV_GROUPS)],
        axis=1).astype(BF16)
    p, gates = _norm_proj(h, norm_g.reshape(1, d), w_main, w_gate, tm=tm, tn=512)
    pos = cmp_pos.astype(F32)
    w1 = cmp_w1.astype(BF16)
    w2 = cmp_w2.astype(BF16)
    cos, sin = _rope_tables(np.arange(s))
    ng = s // CMP_STRIDE
    cosc, sinc = _rope_tables(np.arange(ng) * CMP_STRIDE + CMP_LEN - 1)
    kc, vc, ks, kw = _nsa_prep(p, pos, w1, w2, cos, sin, cosc, sinc,
                               kc_off=8, vc_off=10, ks_off=12, kw_off=16)
    n_cmp = (s - CMP_LEN) // CMP_STRIDE + 1
    n_sel = s // SEL_LEN
    cs = np.arange(n_cmp) * CMP_STRIDE
    ss = np.arange(n_sel) * SEL_LEN
    cover_t = np.zeros((HEAD_DIM, ng), np.float32)
    cover_t[:n_sel, :n_cmp] = (cs[None, :] < ss[:, None] + SEL_LEN) & (cs[None, :] + CMP_LEN > ss[:, None])
    o_c = _nsa(p, kc, vc, ks, kw, gates, cos, sin, jnp.asarray(cover_t, BF16),
               vs_off=14, vw_off=18, tq=128, tk=min(512, s))
    o_d = _hgrn(p, lower_bound.reshape(1, -1).astype(F32), head_norm.reshape(1, -1),
                jnp.asarray(_hgrn_tables(), BF16), q_off=20, f_off=28, i_off=36, g_off=44)
    w_o = w_out.astype(BF16)
    return _proj_res(h, [(o_c, w_o[:cw]), (o_d, w_o[cw:])], tm=tm, tn=512)


def _ffn(h, norm_g, w_up, conv_w, conv_b, w_down):
    b, s, d = h.shape
    f = w_down.shape[0]
    tm = _row_tile(s)
    act = _ffn_up(h, norm_g.reshape(1, d), w_up[:, :f].astype(BF16), w_up[:, f:].astype(BF16),
                  conv_w, conv_b.reshape(1, f), tm=tm, tn=512)
    return _proj_res(h, [(act, w_down.astype(BF16))], tm=tm, tn=256)


def kernel(x, norm_mix, norm_ffn, norm_final, ab_w_in, ab_conv_w, ab_conv_b, ab_gate_b, ab_head_norm, ab_w_out, cd_w_in, cd_cmp_pos, cd_cmp_w1, cd_cmp_w2, hgrn_gamma, cd_head_norm, cd_w_out, ffn_w_up, ffn_conv_w, ffn_conv_b, ffn_w_down):
    depth = norm_mix.shape[0]
    sm = jax.nn.softmax(hgrn_gamma.astype(F32), axis=0)
    lb_all = jnp.cumsum(sm, axis=0) - sm[0]
    h = x
    for layer in range(depth):
        j = layer // 2
        if layer % 2 == 0:
            h = _layer_ab(h, norm_mix[layer], ab_w_in[j], ab_conv_w[j], ab_conv_b[j], ab_gate_b[j],
                          ab_head_norm[j], ab_w_out[j])
        else:
            h = _layer_cd(h, norm_mix[layer], cd_w_in[j], cd_cmp_pos[j], cd_cmp_w1[j], cd_cmp_w2[j],
                          lb_all[layer], cd_head_norm[j], cd_w_out[j])
        h = _ffn(h, norm_ffn[layer], ffn_w_up[layer], ffn_conv_w[layer], ffn_conv_b[layer], ffn_w_down[layer])
    return _rmsnorm(h, norm_final.reshape(1, -1), tm=min(512, h.shape[1]))
```

```python
import functools

import numpy as np
import jax
import jax.numpy as jnp
from jax import lax
from jax.experimental import pallas as pl
from jax.experimental.pallas import tpu as pltpu

F32 = jnp.float32
BF16 = jnp.bfloat16

HEAD_DIM = 128
N_HEADS = 8
KV_GROUPS = 2
GROUP_HEADS = N_HEADS // KV_GROUPS
A_CONV = 4
CMP_LEN = 32
CMP_STRIDE = 16
SEL_LEN = 64
SEL_SHIFT = 6
N_SEL = 8
N_LOCAL = 2
FORCE_BONUS = 1000.0
WINDOW = 256
FFN_CONV = 3
ROPE_THETA = 500000.0
ROPE_DIM = HEAD_DIM // 4
RMS_EPS = 1e-6
SCALE = HEAD_DIM ** -0.5
LOG2_E = float(np.log2(np.e))

CHUNK = 128
HEADS_PER_STEP = 4
HGRN_LEVELS = (64, 32, 16, 8, 4, 2, 1)
FFN_COL_TILES = 4
OUT_ROW_TILE = 512
OUT_COLS = 512
PROJ_COL_TILES = 4
MXU_COLS = 256
VMEM_LIMIT = 56 * 1024 * 1024


def _cparams(sem):
    return pltpu.CompilerParams(dimension_semantics=sem, vmem_limit_bytes=VMEM_LIMIT)


def _dot(a, b):
    return jnp.dot(a, b, preferred_element_type=F32)


def _dot_nt(a, b):
    return lax.dot_general(a, b, (((1,), (1,)), ((), ())), preferred_element_type=F32)


def _dot_split(a, b01, terms):
    acc = None
    rem = a
    for _ in range(terms):
        piece = rem.astype(BF16)
        part = _dot(piece, b01)
        acc = part if acc is None else acc + part
        rem = rem - piece.astype(F32)
    return acc


def _dot_split_rhs(a01, b, terms):
    acc = None
    rem = b
    for _ in range(terms):
        piece = rem.astype(BF16)
        part = _dot(a01, piece)
        acc = part if acc is None else acc + part
        rem = rem - piece.astype(F32)
    return acc


def _dot_nt_split(a01, b, terms):
    acc = None
    rem = b
    for _ in range(terms):
        piece = rem.astype(BF16)
        part = _dot_nt(a01, piece)
        acc = part if acc is None else acc + part
        rem = rem - piece.astype(F32)
    return acc


def _softplus(z):
    return jnp.maximum(z, 0.0) + jnp.log(1.0 + jnp.exp(-jnp.abs(z)))


def _log_sigmoid(z):
    return -_softplus(-z)


def _sigmoid(z):
    return 1.0 / (1.0 + jnp.exp(-z))


def _iota(shape, dim):
    return lax.broadcasted_iota(jnp.int32, shape, dim)


def _rope_swap(x):
    lane = _iota(x.shape, x.ndim - 1)
    up = pltpu.roll(x, HEAD_DIM - ROPE_DIM // 2, axis=x.ndim - 1)
    down = pltpu.roll(x, ROPE_DIM // 2, axis=x.ndim - 1)
    return jnp.where(lane < ROPE_DIM // 2, up, down)


def _rope_tables(pos):
    half = ROPE_DIM // 2
    freqs = ROPE_THETA ** (-np.arange(half, dtype=np.float32) / half)
    ang = jnp.asarray(pos, F32)[:, None] * jnp.asarray(freqs, F32)[None, :]
    cos, sin = jnp.cos(ang), jnp.sin(ang)
    n = ang.shape[0]
    cos_t = jnp.concatenate([cos, cos, jnp.ones((n, HEAD_DIM - ROPE_DIM), F32)], axis=1)
    sin_t = jnp.concatenate([-sin, sin, jnp.zeros((n, HEAD_DIM - ROPE_DIM), F32)], axis=1)
    return cos_t, sin_t


def _rms_to_bf16(h_ref, g_ref):
    x = h_ref[...]
    ms = jnp.mean(x * x, axis=-1, keepdims=True)
    return (x * lax.rsqrt(ms + RMS_EPS) * g_ref[...]).astype(BF16)


def _norm_proj_kernel(h_ref, g_ref, w_ref, wg_ref, o_ref, og_ref, u_scr):
    @pl.when(pl.program_id(2) == 0)
    def _():
        u = _rms_to_bf16(h_ref, g_ref)
        u_scr[...] = u
        og_ref[...] = _dot(u, wg_ref[...])

    u = u_scr[...]
    tn = w_ref.shape[1]
    for c0 in range(0, tn, MXU_COLS):
        width = min(MXU_COLS, tn - c0)
        res = _dot(u, w_ref[:, c0:c0 + width])
        for c in range(width // HEAD_DIM):
            o_ref[c0 // HEAD_DIM + c] = res[:, c * HEAD_DIM:(c + 1) * HEAD_DIM].astype(o_ref.dtype)


def _norm_proj(h, g, w, wg, *, tm, tn):
    b, s, d = h.shape
    n = w.shape[1]
    ng = wg.shape[1]
    ncb = tn // HEAD_DIM
    return pl.pallas_call(
        _norm_proj_kernel,
        grid=(b, s // tm, n // tn),
        in_specs=[
            pl.BlockSpec((None, tm, d), lambda bi, si, j: (bi, si, 0)),
            pl.BlockSpec((1, d), lambda bi, si, j: (0, 0)),
            pl.BlockSpec((d, tn), lambda bi, si, j: (0, j)),
            pl.BlockSpec((d, ng), lambda bi, si, j: (0, 0)),
        ],
        out_specs=[
            pl.BlockSpec((None, ncb, tm, HEAD_DIM), lambda bi, si, j: (bi, j, si, 0)),
            pl.BlockSpec((None, tm, ng), lambda bi, si, j: (bi, si, 0)),
        ],
        out_shape=[
            jax.ShapeDtypeStruct((b, n // HEAD_DIM, s, HEAD_DIM), BF16),
            jax.ShapeDtypeStruct((b, s, ng), F32),
        ],
        scratch_shapes=[pltpu.VMEM((tm, d), BF16)],
        compiler_params=_cparams(("parallel", "parallel", "arbitrary")),
        name="norm_proj",
    )(h, g, w, wg)


def _ffn_up_kernel(u_ref, wg_ref, wu_ref, cw_ref, cb_ref, o_ref, halo_scr):
    si = pl.program_id(1)
    j = pl.program_id(2)
    tm, tn = o_ref.shape

    @pl.when(si == 0)
    def _():
        halo_scr[j] = jnp.zeros(halo_scr.shape[1:], F32)

    u = u_ref[...]
    cw = cw_ref[...]
    cb = cb_ref[...]
    for c0 in range(0, tn, MXU_COLS):
        cols = slice(c0, min(c0 + MXU_COLS, tn))
        gate = _dot(u, wg_ref[:, cols])
        up = _dot(u, wu_ref[:, cols])
        ext = jnp.concatenate([halo_scr[j, :, cols], gate], axis=0)
        halo_scr[j, :, cols] = gate[tm - 8:, :]
        conv = cb[:, cols] + cw[FFN_CONV - 1:FFN_CONV, cols] * gate
        for k in range(FFN_CONV - 1):
            off = 8 - (FFN_CONV - 1) + k
            conv = conv + cw[k:k + 1, cols] * ext[off:off + tm, :]
        o_ref[:, cols] = (conv * _sigmoid(conv) * up).astype(o_ref.dtype)


def _ffn_up(u, wg, wu, cw, cb, *, tm, tn):
    b, s, d = u.shape
    f = wg.shape[1]
    return pl.pallas_call(
        _ffn_up_kernel,
        grid=(b, s // tm, f // tn),
        in_specs=[
            pl.BlockSpec((None, tm, d), lambda bi, si, j: (bi, si, 0)),
            pl.BlockSpec((d, tn), lambda bi, si, j: (0, j)),
            pl.BlockSpec((d, tn), lambda bi, si, j: (0, j)),
            pl.BlockSpec((FFN_CONV, tn), lambda bi, si, j: (0, j)),
            pl.BlockSpec((1, tn), lambda bi, si, j: (0, j)),
        ],
        out_specs=pl.BlockSpec((None, tm, tn), lambda bi, si, j: (bi, si, j)),
        out_shape=jax.ShapeDtypeStruct((b, s, f), BF16),
        scratch_shapes=[pltpu.VMEM((f // tn, 8, tn), F32)],
        compiler_params=_cparams(("parallel", "arbitrary", "arbitrary")),
        name="ffn_up",
    )(u, wg, wu, cw, cb)


def _out_proj_norm_kernel(h_ref, a1_ref, w1_ref, a2_ref, w2_ref, g_ref, ho_ref, u_ref):
    tm, d = ho_ref.shape
    a1 = a1_ref[...]
    a2 = a2_ref[...]
    ss = jnp.zeros((tm, 1), F32)
    for c0 in range(0, d, OUT_COLS):
        cols = slice(c0, c0 + OUT_COLS)
        acc = h_ref[:, cols] + _dot(a1, w1_ref[:, cols]) + _dot(a2, w2_ref[:, cols])
        ho_ref[:, cols] = acc
        ss = ss + jnp.sum(acc * acc, axis=1, keepdims=True)
    inv = lax.rsqrt(ss * (1.0 / d) + RMS_EPS)
    for c0 in range(0, d, OUT_COLS):
        cols = slice(c0, c0 + OUT_COLS)
        u_ref[:, cols] = (ho_ref[:, cols] * inv * g_ref[:, cols]).astype(u_ref.dtype)


def _out_proj_norm(h, a1, w1, a2, w2, g, *, tm):
    b, s, d = h.shape
    k1 = a1.shape[-1]
    k2 = a2.shape[-1]
    rows = lambda k: pl.BlockSpec((None, tm, k), lambda bi, si: (bi, si, 0))
    whole = lambda k: pl.BlockSpec((k, d), lambda bi, si: (0, 0))
    return pl.pallas_call(
        _out_proj_norm_kernel,
        grid=(b, s // tm),
        in_specs=[rows(d), rows(k1), whole(k1), rows(k2), whole(k2), pl.BlockSpec((1, d), lambda bi, si: (0, 0))],
        out_specs=[rows(d), rows(d)],
        out_shape=[jax.ShapeDtypeStruct((b, s, d), F32), jax.ShapeDtypeStruct((b, s, d), BF16)],
        compiler_params=_cparams(("parallel", "parallel")),
        name="out_proj_norm",
    )(h, a1, w1, a2, w2, g)


def _proj_res_kernel(*refs):
    h_ref, o_ref = refs[0], refs[-1]
    ops = refs[1:-1]
    acc = h_ref[...]
    for i in range(0, len(ops), 2):
        acc = acc + _dot(ops[i][...], ops[i + 1][...])
    o_ref[...] = acc


def _proj_res(h, pairs, *, tm, tn):
    b, s, d = h.shape
    in_specs = [pl.BlockSpec((None, tm, tn), lambda bi, si, j: (bi, si, j))]
    args = [h]
    for a, w in pairs:
        k = a.shape[-1]
        in_specs.append(pl.BlockSpec((None, tm, k), lambda bi, si, j: (bi, si, 0)))
        in_specs.append(pl.BlockSpec((k, tn), lambda bi, si, j: (0, j)))
        args += [a, w]
    return pl.pallas_call(
        _proj_res_kernel,
        grid=(b, s // tm, d // tn),
        in_specs=in_specs,
        out_specs=pl.BlockSpec((None, tm, tn), lambda bi, si, j: (bi, si, j)),
        out_shape=jax.ShapeDtypeStruct((b, s, d), F32),
        compiler_params=_cparams(("parallel", "parallel", "arbitrary")),
        name="proj_res",
    )(*args)


def _rmsnorm_kernel(h_ref, g_ref, o_ref):
    x = h_ref[...]
    ms = jnp.mean(x * x, axis=-1, keepdims=True)
    o_ref[...] = x * lax.rsqrt(ms + RMS_EPS) * g_ref[...]


def _rmsnorm(h, g, *, tm):
    b, s, d = h.shape
    return pl.pallas_call(
        _rmsnorm_kernel,
        grid=(b, s // tm),
        in_specs=[pl.BlockSpec((None, tm, d), lambda bi, si: (bi, si, 0)),
                  pl.BlockSpec((1, d), lambda bi, si: (0, 0))],
        out_specs=pl.BlockSpec((None, tm, d), lambda bi, si: (bi, si, 0)),
        out_shape=jax.ShapeDtypeStruct((b, s, d), F32),
        compiler_params=_cparams(("parallel", "parallel")),
        name="final_norm",
    )(h, g)


def _mlstm_kernel(q_ref, k_ref, v_ref, og_ref, ig_ref, fg_ref, bi_ref, bf_ref,
                  cwq_ref, cwk_ref, cbq_ref, cbk_ref, hn_ref, out_ref,
                  pad_scr, qc_scr, kc_scr, c_scr, b_scr, d_scr):
    nh, s, _ = q_ref.shape
    nch = s // CHUNK

    def conv_silu(src_ref, cw_ref, cb_ref, dst_ref, hh, scale):
        lanes = slice(hh * HEAD_DIM, (hh + 1) * HEAD_DIM)
        pad_scr[0:8, :] = jnp.zeros((8, HEAD_DIM), F32)
        for p in range(nch):
            pad_scr[8 + p * CHUNK:8 + (p + 1) * CHUNK, :] = src_ref[hh, p * CHUNK:(p + 1) * CHUNK, :].astype(F32)
        cw = cw_ref[:, lanes]
        for p in range(nch):
            acc = cb_ref[:, lanes] + cw[0:1, :] * pad_scr[pl.ds(8 - (A_CONV - 1) + p * CHUNK, CHUNK), :]
            for j in range(1, A_CONV):
                acc = acc + cw[j:j + 1, :] * pad_scr[pl.ds(8 - (A_CONV - 1) + j + p * CHUNK, CHUNK), :]
            y = acc * _sigmoid(acc)
            if scale != 1.0:
                y = y * scale
            dst_ref[hh, p * CHUNK:(p + 1) * CHUNK, :] = y.astype(BF16)

    r_i = _iota((CHUNK, CHUNK), 0)
    c_i = _iota((CHUNK, CHUNK), 1)
    upper = (r_i <= c_i).astype(BF16)
    for hh in range(nh):
        conv_silu(q_ref, cwq_ref, cbq_ref, qc_scr, hh, 1.0)
        conv_silu(k_ref, cwk_ref, cbk_ref, kc_scr, hh, SCALE)
        log_f = _log_sigmoid(fg_ref[hh] + bf_ref[hh])
        log_i = ig_ref[hh] + bi_ref[hh]
        b_all = _dot_split(log_f, upper, 3)
        b_scr[hh] = b_all
        d_scr[hh] = log_i - b_all
    c_scr[...] = jnp.zeros(c_scr.shape, F32)
    causal = c_i <= r_i
    ones_blk = jnp.ones((CHUNK, HEAD_DIM), BF16)

    def head_chunk(hh, c, t0, m):
        lanes = slice(hh * HEAD_DIM, (hh + 1) * HEAD_DIM)
        q = qc_scr[hh, pl.ds(t0, CHUNK), :]
        k = kc_scr[hh, pl.ds(t0, CHUNK), :]
        v_aug = jnp.concatenate([v_ref[hh, pl.ds(t0, CHUNK), :], ones_blk], axis=1)
        b_row = b_scr[hh, pl.ds(c, 1), :]
        d_row = d_scr[hh, pl.ds(c, 1), :]
        b_col = jnp.transpose(jnp.broadcast_to(b_row, (CHUNK, CHUNK)))
        d_col = jnp.transpose(jnp.broadcast_to(d_row, (CHUNK, CHUNK)))
        dmat = jnp.where(causal, b_col + d_row, -jnp.inf)
        g = b_col + m
        m_row = jnp.maximum(g, jnp.max(dmat, axis=1, keepdims=True))
        w = jnp.exp(dmat - m_row) * _dot_nt(q, k)
        w_state = jnp.exp(g - m_row)
        c_old = c_scr[hh]
        inter = _dot(q, c_old.astype(BF16))
        intra = _dot(w.astype(BF16), v_aug)
        tot = jnp.concatenate([w_state, w_state], axis=1) * inter + intra
        num = tot[:, :HEAD_DIM]
        den = tot[:, HEAD_DIM:]
        hv = num / jnp.maximum(jnp.abs(den), jnp.exp(-m_row))
        y = hv * lax.rsqrt(jnp.mean(hv * hv, axis=-1, keepdims=True) + RMS_EPS) * hn_ref[:, lanes]
        y = y * _sigmoid(og_ref[hh, pl.ds(t0, CHUNK), :].astype(F32))
        out_ref[pl.ds(t0, CHUNK), lanes] = y.astype(out_ref.dtype)
        b_last = b_col[CHUNK - 1:CHUNK, :]
        w_new = b_last + d_col
        m_new = jnp.maximum(b_last + m, jnp.max(w_new, axis=0, keepdims=True))
        decay = jnp.exp(b_last + m - m_new)
        kw = jnp.exp(w_new - m_new) * k.astype(F32)
        kw_t = jnp.transpose(kw).astype(BF16)
        c_scr[hh] = jnp.concatenate([decay, decay], axis=1) * c_old + _dot(kw_t, v_aug)
        return m_new

    def body(c, ms):
        t0 = pl.multiple_of(c * CHUNK, CHUNK)
        return tuple(head_chunk(hh, c, t0, ms[hh]) for hh in range(nh))

    lax.fori_loop(0, nch, body, tuple(jnp.zeros((1, CHUNK), F32) for _ in range(nh)))


def _mlstm(p, gi, gf, bias_i, bias_f, cw, cb, hn):
    b, _, s, _ = p.shape
    nch = s // CHUNK
    nh = HEADS_PER_STEP
    nblk = N_HEADS // nh
    w = nh * HEAD_DIM
    slab = lambda off: pl.BlockSpec((None, nh, s, HEAD_DIM), lambda bi, hi: (bi, off + hi, 0, 0))
    gate = pl.BlockSpec((None, nh, nch, CHUNK), lambda bi, hi: (bi, hi, 0, 0))
    hb = pl.BlockSpec((nh, 1, CHUNK), lambda bi, hi: (hi, 0, 0))
    return pl.pallas_call(
        _mlstm_kernel,
        grid=(b, nblk),
        in_specs=[
            slab(0), slab(nblk), slab(2 * nblk), slab(3 * nblk), gate, gate, hb, hb,
            pl.BlockSpec((A_CONV, w), lambda bi, hi: (0, hi)),
            pl.BlockSpec((A_CONV, w), lambda bi, hi: (0, nblk + hi)),
            pl.BlockSpec((1, w), lambda bi, hi: (0, hi)),
            pl.BlockSpec((1, w), lambda bi, hi: (0, nblk + hi)),
            pl.BlockSpec((1, w), lambda bi, hi: (0, hi)),
        ],
        out_specs=pl.BlockSpec((None, s, w), lambda bi, hi: (bi, 0, hi)),
        out_shape=jax.ShapeDtypeStruct((b, s, N_HEADS * HEAD_DIM), BF16),
        scratch_shapes=[
            pltpu.VMEM((s + 8, HEAD_DIM), F32),
            pltpu.VMEM((nh, s, HEAD_DIM), BF16),
            pltpu.VMEM((nh, s, HEAD_DIM), BF16),
            pltpu.VMEM((nh, HEAD_DIM, 2 * HEAD_DIM), F32),
            pltpu.VMEM((nh, nch, CHUNK), F32),
            pltpu.VMEM((nh, nch, CHUNK), F32),
        ],
        compiler_params=_cparams(("parallel", "parallel")),
        name="mlstm",
    )(p, p, p, p, gi, gf, bias_i, bias_f, cw, cw, cb, cb, hn)


def _sb_kernel(q_ref, k_ref, v_ref, o_ref):
    qi = pl.program_id(2)
    sub = q_ref.shape[0] // 2
    r_i = _iota((sub, sub), 0)
    c_i = _iota((sub, sub), 1)
    suffix = (r_i > c_i).astype(BF16)
    strict = c_i < r_i
    q_lo = q_ref[0:sub, :]
    q_hi = q_ref[sub:2 * sub, :]

    def sub_block(q, k, v, run, diag):
        z2 = _dot_nt(q, k)
        sp2 = jnp.maximum(z2, 0.0) + jnp.log2(1.0 + jnp.exp2(-jnp.abs(z2)))
        log_keep = -sp2
        if diag:
            log_keep = jnp.where(strict, log_keep, 0.0)
        later = _dot(log_keep.astype(BF16), suffix) + run
        a = jnp.exp2(z2 - sp2 + later)
        if diag:
            a = jnp.where(strict, a, 0.0)
        return _dot(a.astype(BF16), v), jnp.sum(log_keep, axis=1, keepdims=True)

    def tile(kt, carry, diag):
        acc_lo, run_lo, acc_hi, run_hi = carry
        k0 = pl.multiple_of(kt * 2 * sub, 2 * sub)
        k_a = k_ref[pl.ds(k0, sub), :]
        v_a = v_ref[pl.ds(k0, sub), :]
        k_b = k_ref[pl.ds(k0 + sub, sub), :]
        v_b = v_ref[pl.ds(k0 + sub, sub), :]
        o_hb, s_hb = sub_block(q_hi, k_b, v_b, run_hi, diag)
        o_ha, s_ha = sub_block(q_hi, k_a, v_a, run_hi + s_hb, False)
        if diag:
            o_la, s_la = sub_block(q_lo, k_a, v_a, run_lo, True)
            acc_lo = acc_lo + o_la
            run_lo = run_lo + s_la
        else:
            o_lb, s_lb = sub_block(q_lo, k_b, v_b, run_lo, False)
            o_la, s_la = sub_block(q_lo, k_a, v_a, run_lo + s_lb, False)
            acc_lo = acc_lo + (o_lb + o_la)
            run_lo = run_lo + (s_lb + s_la)
        return acc_lo, run_lo, acc_hi + (o_hb + o_ha), run_hi + (s_hb + s_ha)

    zero_acc = jnp.zeros((sub, HEAD_DIM), F32)
    zero_run = jnp.zeros((sub, 1), F32)
    carry = tile(qi, (zero_acc, zero_run, zero_acc, zero_run), True)
    acc_lo, _, acc_hi, _ = lax.fori_loop(0, qi, lambda it, cr: tile(qi - 1 - it, cr, False), carry)
    o_ref[0:sub, :] = acc_lo.astype(o_ref.dtype)
    o_ref[sub:2 * sub, :] = acc_hi.astype(o_ref.dtype)


def _stick_breaking(p, *, q_off, k_off, v_off, tq):
    b, _, s, _ = p.shape
    return pl.pallas_call(
        _sb_kernel,
        grid=(b, N_HEADS, s // tq),
        in_specs=[
            pl.BlockSpec((None, None, tq, HEAD_DIM), lambda bi, hi, qi: (bi, q_off + hi, qi, 0)),
            pl.BlockSpec((None, None, s, HEAD_DIM), lambda bi, hi, qi: (bi, k_off + hi, 0, 0)),
            pl.BlockSpec((None, None, s, HEAD_DIM), lambda bi, hi, qi: (bi, v_off + hi, 0, 0)),
        ],
        out_specs=pl.BlockSpec((None, tq, HEAD_DIM), lambda bi, hi, qi: (bi, qi, hi)),
        out_shape=jax.ShapeDtypeStruct((b, s, N_HEADS * HEAD_DIM), BF16),
        compiler_params=_cparams(("parallel", "parallel", "arbitrary")),
        name="stick_breaking",
    )(p, p, p)


def _gelu_tanh(x):
    return 0.5 * x * (1.0 + jnp.tanh(np.sqrt(2.0 / np.pi).astype(np.float32) * (x + 0.044715 * (x * x * x))))


def _nsa_prep_kernel(kcs_ref, vcs_ref, ks_ref, kw_ref, pos_ref, w1_ref, w2_ref,
                     cos_ref, sin_ref, cosc_ref, sinc_ref,
                     kc_out, vc_out, ks_out, kw_out, f32_scr):
    s = ks_ref.shape[0]
    ng = s // CMP_STRIDE

    def compress(src_ref, idx):
        f32_scr[...] = src_ref[...].astype(F32)
        xa = jnp.zeros((ng, HEAD_DIM), F32)
        xb = jnp.zeros((ng, HEAD_DIM), F32)
        for l in range(CMP_STRIDE):
            rows = f32_scr[pl.ds(l, ng, stride=CMP_STRIDE), :]
            xa = xa + _dot((rows + pos_ref[idx, l:l + 1, :]).astype(BF16), w1_ref[idx, l])
            xb = xb + _dot((rows + pos_ref[idx, CMP_STRIDE + l:CMP_STRIDE + l + 1, :]).astype(BF16),
                           w1_ref[idx, CMP_STRIDE + l])
        hid = _gelu_tanh(xa + pltpu.roll(xb, ng - 1, axis=0))
        return _dot(hid.astype(BF16), w2_ref[idx])

    kc = compress(kcs_ref, 0)
    kc_out[...] = (kc * cosc_ref[...] + _rope_swap(kc) * sinc_ref[...]).astype(kc_out.dtype)
    vc_out[...] = compress(vcs_ref, 1).astype(vc_out.dtype)
    for src, dst in ((ks_ref, ks_out), (kw_ref, kw_out)):
        x = src[...].astype(F32)
        dst[...] = (x * cos_ref[...] + _rope_swap(x) * sin_ref[...]).astype(dst.dtype)


def _nsa_prep(p, pos, w1, w2, cos, sin, cosc, sinc, *, kc_off, vc_off, ks_off, kw_off):
    b, _, s, _ = p.shape
    ng = s // CMP_STRIDE
    slab = lambda off: pl.BlockSpec((None, None, s, HEAD_DIM), lambda bi, gi: (bi, off + gi, 0, 0))
    full = lambda a: pl.BlockSpec(a.shape, lambda bi, gi: (0,) * a.ndim)
    out_c = pl.BlockSpec((None, None, ng, HEAD_DIM), lambda bi, gi: (bi, gi, 0, 0))
    out_s = pl.BlockSpec((None, None, s, HEAD_DIM), lambda bi, gi: (bi, gi, 0, 0))
    return pl.pallas_call(
        _nsa_prep_kernel,
        grid=(b, KV_GROUPS),
        in_specs=[slab(kc_off), slab(vc_off), slab(ks_off), slab(kw_off),
                  full(pos), full(w1), full(w2), full(cos), full(sin), full(cosc), full(sinc)],
        out_specs=[out_c, out_c, out_s, out_s],
        out_shape=[jax.ShapeDtypeStruct((b, KV_GROUPS, ng, HEAD_DIM), BF16),
                   jax.ShapeDtypeStruct((b, KV_GROUPS, ng, HEAD_DIM), BF16),
                   jax.ShapeDtypeStruct((b, KV_GROUPS, s, HEAD_DIM), BF16),
                   jax.ShapeDtypeStruct((b, KV_GROUPS, s, HEAD_DIM), BF16)],
        scratch_shapes=[pltpu.VMEM((s, HEAD_DIM), F32)],
        compiler_params=_cparams(("parallel", "parallel")),
        name="nsa_prep",
    )(p, p, p, p, pos, w1, w2, cos, sin, cosc, sinc)


def _masked_softmax(sc, mask):
    sc = jnp.where(mask, sc, -jnp.inf)
    m = jnp.max(sc, axis=-1, keepdims=True)
    m = jnp.where(m > -jnp.inf, m, 0.0)
    e = jnp.exp(sc - m)
    return e / jnp.maximum(jnp.sum(e, axis=-1, keepdims=True), 1e-30)


def _nsa_kernel(q_ref, kc_ref, vc_ref, ks_ref, vs_ref, kw_ref, vw_ref, gate_ref, cos_ref, sin_ref,
                cover_t_ref, o_ref, *, tk):
    qi = pl.program_id(2)
    r = q_ref.shape[0]
    tq = q_ref.shape[1]
    s_len = ks_ref.shape[0]
    n_cmp = (s_len - CMP_LEN) // CMP_STRIDE + 1
    t0 = qi * tq

    qf = q_ref[...].astype(F32).reshape(r * tq, HEAD_DIM)
    cos = jnp.concatenate([cos_ref[...]] * r, axis=0)
    sin = jnp.concatenate([sin_ref[...]] * r, axis=0)
    qb = (qf * cos + _rope_swap(qf) * sin).astype(BF16)

    t_col = t0 + _iota((tq, HEAD_DIM), 0)
    lane = _iota((tq, HEAD_DIM), 1)

    ncl = kc_ref.shape[0]
    sc = _dot_nt(qb, kc_ref[...]).reshape(r, tq, ncl)
    cmp_ok = ((lane * CMP_STRIDE + (CMP_LEN - 1) <= t_col) & (lane < n_cmp))[:, :ncl]
    p_cmp = _masked_softmax(sc, cmp_ok[None])
    o_cmp = _dot(p_cmp.reshape(r * tq, ncl).astype(BF16), vc_ref[...]).reshape(r, tq, HEAD_DIM)

    band = WINDOW + tq
    w0 = pl.multiple_of(jnp.maximum(t0 - WINDOW, 0), tq)
    k_win = kw_ref[pl.ds(w0, band), :]
    v_win = vw_ref[pl.ds(w0, band), :]
    diff = (t0 + _iota((tq, band), 0)) - (w0 + _iota((tq, band), 1))
    win_ok = (diff >= 0) & (diff < WINDOW)
    sw = _dot_nt(qb, k_win).reshape(r, tq, band)
    p_win = _masked_softmax(sw, win_ok[None])
    o_win = _dot(p_win.reshape(r * tq, band).astype(BF16), v_win).reshape(r, tq, HEAD_DIM)

    p_sum = p_cmp[0]
    for h in range(1, r):
        p_sum = p_sum + p_cmp[h]
    n_blk = s_len // SEL_LEN
    p_slc_t = _dot_nt_split(cover_t_ref[...], p_sum, 3)[:n_blk, :]
    blk = _iota((n_blk, tq), 0)
    blk_t = (t0 + _iota((n_blk, tq), 1)) >> SEL_SHIFT
    started = blk <= blk_t
    forced = (blk == 0) | (started & (blk_t - blk < N_LOCAL))
    score = jnp.where(started, p_slc_t + jnp.where(forced, FORCE_BONUS, 0.0), -jnp.inf)
    rank = jnp.zeros((n_blk, tq), F32)
    for jp in range(n_blk):
        row = score[jp:jp + 1, :]
        beats = (row > score) | ((row == score) & (blk > jp))
        rank = rank + jnp.where(beats, 1.0, 0.0)
    sel_t = jnp.where((rank < N_SEL) & started, 1.0, 0.0)
    sel_t = jnp.concatenate([sel_t, jnp.zeros((HEAD_DIM - n_blk, tq), F32)], axis=0)
    sel = jnp.transpose(sel_t).astype(BF16)

    e_row = _iota((HEAD_DIM, tk), 0)
    e_blk = _iota((HEAD_DIM, tk), 1) >> SEL_SHIFT
    s_loc = _iota((tq, tk), 1)
    t_q = t0 + _iota((tq, tk), 0)
    n_chain = r
    hc = r // n_chain
    q_chain = [qb[c * hc * tq:(c + 1) * hc * tq, :] for c in range(n_chain)]
    ones_blk = jnp.ones((tk, HEAD_DIM), BF16)

    def sel_tile(kt, carry):
        k0 = pl.multiple_of(kt * tk, tk)
        k = ks_ref[pl.ds(k0, tk), :]
        v_aug = jnp.concatenate([vs_ref[pl.ds(k0, tk), :], ones_blk], axis=1)
        expand = (e_row == e_blk + kt * (tk // SEL_LEN)).astype(BF16)
        ok = (_dot(sel, expand) > 0.5) & (k0 + s_loc <= t_q)
        bias = jnp.where(ok, 0.0, -jnp.inf)[None]
        new = []
        for c in range(n_chain):
            m, acc = carry[c]
            sc_t = _dot_nt(q_chain[c], k).reshape(hc, tq, tk) + bias
            m_new = jnp.maximum(m, jnp.max(sc_t, axis=-1, keepdims=True))
            alpha = jnp.exp(m - m_new)
            e = jnp.exp(sc_t - m_new)
            pv = _dot(e.reshape(hc * tq, tk).astype(BF16), v_aug).reshape(hc, tq, 2 * HEAD_DIM)
            new.append((m_new, alpha * acc + pv))
        return tuple(new)

    n_tiles = (t0 + tq + tk - 1) // tk
    init = tuple((jnp.full((hc, tq, 1), -jnp.inf, F32), jnp.zeros((hc, tq, 2 * HEAD_DIM), F32))
                 for _ in range(n_chain))
    chains = lax.fori_loop(0, n_tiles, sel_tile, init)

    gates = _sigmoid(gate_ref[...])
    for h in range(r):
        acc_s = chains[h // hc][1][h % hc]
        o_slc = acc_s[:, :HEAD_DIM] / jnp.maximum(acc_s[:, HEAD_DIM:], 1e-30)
        out = (gates[:, 3 * h:3 * h + 1] * o_cmp[h] + gates[:, 3 * h + 1:3 * h + 2] * o_slc
               + gates[:, 3 * h + 2:3 * h + 3] * o_win[h])
        o_ref[:, h * HEAD_DIM:(h + 1) * HEAD_DIM] = out.astype(o_ref.dtype)


def _nsa(p, kc, vc, ks, kw, gates, cos, sin, cover, *, vs_off, vw_off, tq, tk):
    b, _, s, _ = p.shape
    r = GROUP_HEADS
    ng = kc.shape[2]
    kv = lambda: pl.BlockSpec((None, None, s, HEAD_DIM), lambda bi, gi, qi: (bi, gi, 0, 0))
    pslab = lambda off: pl.BlockSpec((None, None, s, HEAD_DIM), lambda bi, gi, qi: (bi, off + gi, 0, 0))
    cmp_spec = pl.BlockSpec((None, None, ng, HEAD_DIM), lambda bi, gi, qi: (bi, gi, 0, 0))
    return pl.pallas_call(
        functools.partial(_nsa_kernel, tk=tk),
        grid=(b, KV_GROUPS, s // tq),
        in_specs=[
            pl.BlockSpec((None, r, tq, HEAD_DIM), lambda bi, gi, qi: (bi, gi, qi, 0)),
            cmp_spec, cmp_spec, kv(), pslab(vs_off), kv(), pslab(vw_off),
            pl.BlockSpec((None, tq, HEAD_DIM), lambda bi, gi, qi: (bi, qi, gi)),
            pl.BlockSpec((tq, HEAD_DIM), lambda bi, gi, qi: (qi, 0)),
            pl.BlockSpec((tq, HEAD_DIM), lambda bi, gi, qi: (qi, 0)),
            pl.BlockSpec(cover.shape, lambda bi, gi, qi: (0, 0)),
        ],
        out_specs=pl.BlockSpec((None, tq, r * HEAD_DIM), lambda bi, gi, qi: (bi, qi, gi)),
        out_shape=jax.ShapeDtypeStruct((b, s, N_HEADS * HEAD_DIM), BF16),
        compiler_params=_cparams(("parallel", "parallel", "arbitrary")),
        name="nsa",
    )(p, kc, vc, ks, p, kw, p, gates, cos, sin, cover)


def _hgrn_tables():
    r = np.arange(CHUNK)[:, None]
    j = np.arange(CHUNK)[None, :]
    mats = []
    for sz in HGRN_LEVELS:
        mid = (r // (2 * sz)) * (2 * sz) + sz
        right = (r // sz) % 2 == 1
        mats.append(np.where(right, (j >= mid) & (j <= r), (j > r) & (j <= mid - 1)))
    mats.append(j <= r)
    mats.append(j > r)
    return np.concatenate(mats, axis=0).astype(np.float32)


def _hgrn_kernel(q_ref, f_ref, i_ref, g_ref, lb_ref, hn_ref, tab_ref, out_ref, st_scr):
    nh, s, _ = q_ref.shape
    nch = s // CHUNK
    nlev = len(HGRN_LEVELS)
    r_i = _iota((CHUNK, CHUNK), 0)
    c_i = _iota((CHUNK, CHUNK), 1)
    row = _iota((CHUNK, HEAD_DIM), 0)
    st_scr[...] = jnp.zeros(st_scr.shape, F32)

    def head_chunk(hh, t0):
        lanes = slice(hh * HEAD_DIM, (hh + 1) * HEAD_DIM)
        lb = lb_ref[:, lanes]
        q = q_ref[hh, pl.ds(t0, CHUNK), :].astype(F32)
        fp = f_ref[hh, pl.ds(t0, CHUNK), :].astype(F32)
        v = i_ref[hh, pl.ds(t0, CHUNK), :]
        log_lb = jnp.log(lb)
        a2 = jnp.log1p(-lb) - _softplus(-fp)
        hi = jnp.maximum(log_lb, a2)
        lo = jnp.minimum(log_lb, a2)
        log_f = hi + jnp.log(1.0 + jnp.exp(lo - hi))
        k = (1.0 - lb) * _sigmoid(-fp)
        x_all = _dot_split_rhs(tab_ref[...], log_f, 2)
        att = jnp.where(r_i == c_i, jnp.sum(q * k, axis=1, keepdims=True), 0.0)
        for lv, sz in enumerate(HGRN_LEVELS):
            sh = sz.bit_length() - 1
            e = jnp.exp(x_all[lv * CHUNK:(lv + 1) * CHUNK, :])
            right = ((row >> sh) & 1) == 1
            qt = jnp.where(right, q * e, 0.0).astype(BF16)
            kt = jnp.where(right, 0.0, k * e).astype(BF16)
            same = (r_i >> (sh + 1)) == (c_i >> (sh + 1))
            att = att + jnp.where(same, _dot_nt(qt, kt), 0.0)
        bc = x_all[nlev * CHUNK:(nlev + 1) * CHUNK, :]
        rev = x_all[(nlev + 1) * CHUNK:(nlev + 2) * CHUNK, :]
        st = st_scr[hh]
        o = _dot(att.astype(BF16), v) + _dot_nt((q * jnp.exp(bc)).astype(BF16), st.astype(BF16))
        y = o * lax.rsqrt(jnp.mean(o * o, axis=-1, keepdims=True) + RMS_EPS) * hn_ref[:, lanes]
        gg = g_ref[hh, pl.ds(t0, CHUNK), :].astype(F32)
        out_ref[pl.ds(t0, CHUNK), lanes] = (y * (gg * _sigmoid(gg))).astype(out_ref.dtype)
        v_t = jnp.transpose(v.astype(F32)).astype(BF16)
        st_scr[hh] = st * jnp.exp(bc[CHUNK - 1:CHUNK, :]) + _dot(v_t, (k * jnp.exp(rev)).astype(BF16))

    def body(c, carry):
        t0 = pl.multiple_of(c * CHUNK, CHUNK)
        for hh in range(nh):
            head_chunk(hh, t0)
        return carry

    lax.fori_loop(0, nch, body, 0)


def _hgrn(p, lb, hn, tab, *, q_off, f_off, i_off, g_off):
    b, _, s, _ = p.shape
    nh = HEADS_PER_STEP
    nblk = N_HEADS // nh
    w = nh * HEAD_DIM
    slab = lambda off: pl.BlockSpec((None, nh, s, HEAD_DIM), lambda bi, hi: (bi, off // nh + hi, 0, 0))
    return pl.pallas_call(
        _hgrn_kernel,
        grid=(b, nblk),
        in_specs=[slab(q_off), slab(f_off), slab(i_off), slab(g_off),
                  pl.BlockSpec((1, w), lambda bi, hi: (0, hi)),
                  pl.BlockSpec((1, w), lambda bi, hi: (0, hi)),
                  pl.BlockSpec(tab.shape, lambda bi, hi: (0, 0))],
        out_specs=pl.BlockSpec((None, s, w), lambda bi, hi: (bi, 0, hi)),
        out_shape=jax.ShapeDtypeStruct((b, s, N_HEADS * HEAD_DIM), BF16),
        scratch_shapes=[pltpu.VMEM((nh, HEAD_DIM, HEAD_DIM), F32)],
        compiler_params=_cparams(("parallel", "parallel")),
        name="hgrn2",
    )(p, p, p, p, lb, hn, tab)


def _pad_cols(w, n):
    return jnp.pad(w, ((0, 0), (0, n - w.shape[1])))


def _row_tile(s):
    return min(1024, s)


def _layer_ab(h, norm_g, w_in, conv_w, conv_b, gate_b, head_norm, w_out, ffn_norm_g):
    b, s, d = h.shape
    aw = N_HEADS * HEAD_DIM
    tm = _row_tile(s)
    g0 = 4 * aw
    g1 = g0 + 2 * N_HEADS
    w_main = jnp.concatenate([w_in[:, :g0], w_in[:, g1:g1 + aw] * (SCALE * LOG2_E), w_in[:, g1 + aw:]],
                             axis=1).astype(BF16)
    w_gate = _pad_cols(w_in[:, g0:g1], HEAD_DIM).astype(BF16)
    p, gates = _norm_proj(h, norm_g.reshape(1, d), w_main, w_gate, tm=tm, tn=w_main.shape[1] // PROJ_COL_TILES)
    nch = s // CHUNK
    g_t = jnp.transpose(gates[:, :, :2 * N_HEADS], (0, 2, 1))
    gi = g_t[:, :N_HEADS].reshape(b, N_HEADS, nch, CHUNK)
    gf = g_t[:, N_HEADS:].reshape(b, N_HEADS, nch, CHUNK)
    bias_i = jnp.broadcast_to(gate_b[:N_HEADS, None, None], (N_HEADS, 1, CHUNK)).astype(F32)
    bias_f = jnp.broadcast_to(gate_b[N_HEADS:, None, None], (N_HEADS, 1, CHUNK)).astype(F32)
    h_a = _mlstm(p, gi, gf, bias_i, bias_f, conv_w, conv_b.reshape(1, -1), head_norm.reshape(1, -1))
    h_b = _stick_breaking(p, q_off=4 * N_HEADS, k_off=5 * N_HEADS, v_off=6 * N_HEADS, tq=min(512, s))
    w_o = w_out.astype(BF16)
    return _out_proj_norm(h, h_a, w_o[:aw], h_b, w_o[aw:], ffn_norm_g.reshape(1, d), tm=min(OUT_ROW_TILE, s))


def _layer_cd(h, norm_g, w_in, cmp_pos, cmp_w1, cmp_w2, lower_bound, head_norm, w_out, ffn_norm_g):
    b, s, d = h.shape
    cw = N_HEADS * HEAD_DIM
    kvw = KV_GROUPS * HEAD_DIM
    tm = _row_tile(s)
    n_gate = 3 * N_HEADS
    g0 = cw + 6 * kvw
    w_main = jnp.concatenate([w_in[:, :cw] * SCALE, w_in[:, cw:g0], w_in[:, g0 + n_gate:]], axis=1).astype(BF16)
    per_group = 3 * GROUP_HEADS
    w_gate = jnp.concatenate(
        [_pad_cols(w_in[:, g0 + g * per_group:g0 + (g + 1) * per_group], HEAD_DIM) for g in range(KV_GROUPS)],
        axis=1).astype(BF16)
    p, gates = _norm_proj(h, norm_g.reshape(1, d), w_main, w_gate, tm=tm, tn=w_main.shape[1] // PROJ_COL_TILES)
    pos = cmp_pos.astype(F32)
    w1 = cmp_w1.astype(BF16)
    w2 = cmp_w2.astype(BF16)
    cos, sin = _rope_tables(np.arange(s))
    ng = s // CMP_STRIDE
    cosc, sinc = _rope_tables(np.arange(ng) * CMP_STRIDE + CMP_LEN - 1)
    kc, vc, ks, kw = _nsa_prep(p, pos, w1, w2, cos, sin, cosc, sinc,
                               kc_off=8, vc_off=10, ks_off=12, kw_off=16)
    n_cmp = (s - CMP_LEN) // CMP_STRIDE + 1
    n_sel = s // SEL_LEN
    cs = np.arange(n_cmp) * CMP_STRIDE
    ss = np.arange(n_sel) * SEL_LEN
    cover_t = np.zeros((HEAD_DIM, ng), np.float32)
    cover_t[:n_sel, :n_cmp] = (cs[None, :] < ss[:, None] + SEL_LEN) & (cs[None, :] + CMP_LEN > ss[:, None])
    o_c = _nsa(p, kc, vc, ks, kw, gates, cos, sin, jnp.asarray(cover_t, BF16),
               vs_off=14, vw_off=18, tq=256, tk=min(512, s))
    o_d = _hgrn(p, lower_bound.reshape(1, -1).astype(F32), head_norm.reshape(1, -1),
                jnp.asarray(_hgrn_tables(), BF16), q_off=20, f_off=28, i_off=36, g_off=44)
    w_o = w_out.astype(BF16)
    return _out_proj_norm(h, o_c, w_o[:cw], o_d, w_o[cw:], ffn_norm_g.reshape(1, d), tm=min(OUT_ROW_TILE, s))


def _ffn(h, u, w_up, conv_w, conv_b, w_down):
    b, s, d = h.shape
    f = w_down.shape[0]
    tm = _row_tile(s)
    act = _ffn_up(u, w_up[:, :f].astype(BF16), w_up[:, f:].astype(BF16),
                  conv_w, conv_b.reshape(1, f), tm=tm, tn=f // FFN_COL_TILES)
    return _proj_res(h, [(act, w_down.astype(BF16))], tm=tm, tn=512)


def kernel(x, norm_mix, norm_ffn, norm_final, ab_w_in, ab_conv_w, ab_conv_b, ab_gate_b, ab_head_norm, ab_w_out, cd_w_in, cd_cmp_pos, cd_cmp_w1, cd_cmp_w2, hgrn_gamma, cd_head_norm, cd_w_out, ffn_w_up, ffn_conv_w, ffn_conv_b, ffn_w_down):
    depth = norm_mix.shape[0]
    sm = jax.nn.softmax(hgrn_gamma.astype(F32), axis=0)
    lb_all = jnp.cumsum(sm, axis=0) - sm[0]
    h = x
    for layer in range(depth):
        j = layer // 2
        if layer % 2 == 0:
            h, u = _layer_ab(h, norm_mix[layer], ab_w_in[j], ab_conv_w[j], ab_conv_b[j], ab_gate_b[j],
                             ab_head_norm[j], ab_w_out[j], norm_ffn[layer])
        else:
            h, u = _layer_cd(h, norm_mix[layer], cd_w_in[j], cd_cmp_pos[j], cd_cmp_w1[j], cd_cmp_w2[j],
                             lb_all[layer], cd_head_norm[j], cd_w_out[j], norm_ffn[layer])
        h = _ffn(h, u, ffn_w_up[layer], ffn_conv_w[layer], ffn_conv_b[layer], ffn_w_down[layer])
    return _rmsnorm(h, norm_final.reshape(1, -1), tm=min(512, h.shape[1]))
```

```python
import functools

import numpy as np
import jax
import jax.numpy as jnp
from jax import lax
from jax.experimental import pallas as pl
from jax.experimental.pallas import tpu as pltpu

F32 = jnp.float32
BF16 = jnp.bfloat16

HEAD_DIM = 128
N_HEADS = 8
KV_GROUPS = 2
GROUP_HEADS = N_HEADS // KV_GROUPS
A_CONV = 4
CMP_LEN = 32
CMP_STRIDE = 16
SEL_LEN = 64
SEL_SHIFT = 6
N_SEL = 8
N_LOCAL = 2
FORCE_BONUS = 1000.0
WINDOW = 256
FFN_CONV = 3
ROPE_THETA = 500000.0
ROPE_DIM = HEAD_DIM // 4
RMS_EPS = 1e-6
SCALE = HEAD_DIM ** -0.5
LOG2_E = float(np.log2(np.e))

CHUNK = 128
HEADS_PER_STEP = 4
HGRN_LEVELS = (64, 32, 16, 8, 4, 2, 1)
FFN_COL_TILES = 4
OUT_ROW_TILE = 512
OUT_COLS = 512
SB_SUB = 256
SB_ROW_GROUPS = 4
SB_DEAD_LOG2 = -160.0
PROJ_COL_TILES = 4
MXU_COLS = 256
VMEM_LIMIT = 56 * 1024 * 1024


def _cparams(sem):
    return pltpu.CompilerParams(dimension_semantics=sem, vmem_limit_bytes=VMEM_LIMIT)


def _dot(a, b):
    return jnp.dot(a, b, preferred_element_type=F32)


def _dot_nt(a, b):
    return lax.dot_general(a, b, (((1,), (1,)), ((), ())), preferred_element_type=F32)


def _dot_split(a, b01, terms):
    acc = None
    rem = a
    for _ in range(terms):
        piece = rem.astype(BF16)
        part = _dot(piece, b01)
        acc = part if acc is None else acc + part
        rem = rem - piece.astype(F32)
    return acc


def _dot_split_rhs(a01, b, terms):
    acc = None
    rem = b
    for _ in range(terms):
        piece = rem.astype(BF16)
        part = _dot(a01, piece)
        acc = part if acc is None else acc + part
        rem = rem - piece.astype(F32)
    return acc


def _dot_nt_split(a01, b, terms):
    acc = None
    rem = b
    for _ in range(terms):
        piece = rem.astype(BF16)
        part = _dot_nt(a01, piece)
        acc = part if acc is None else acc + part
        rem = rem - piece.astype(F32)
    return acc


def _softplus(z):
    return jnp.maximum(z, 0.0) + jnp.log(1.0 + jnp.exp(-jnp.abs(z)))


def _log_sigmoid(z):
    return -_softplus(-z)


def _sigmoid(z):
    return 1.0 / (1.0 + jnp.exp(-z))


def _iota(shape, dim):
    return lax.broadcasted_iota(jnp.int32, shape, dim)


def _rope_swap(x):
    lane = _iota(x.shape, x.ndim - 1)
    up = pltpu.roll(x, HEAD_DIM - ROPE_DIM // 2, axis=x.ndim - 1)
    down = pltpu.roll(x, ROPE_DIM // 2, axis=x.ndim - 1)
    return jnp.where(lane < ROPE_DIM // 2, up, down)


def _rope_tables(pos):
    half = ROPE_DIM // 2
    freqs = ROPE_THETA ** (-np.arange(half, dtype=np.float32) / half)
    ang = jnp.asarray(pos, F32)[:, None] * jnp.asarray(freqs, F32)[None, :]
    cos, sin = jnp.cos(ang), jnp.sin(ang)
    n = ang.shape[0]
    cos_t = jnp.concatenate([cos, cos, jnp.ones((n, HEAD_DIM - ROPE_DIM), F32)], axis=1)
    sin_t = jnp.concatenate([-sin, sin, jnp.zeros((n, HEAD_DIM - ROPE_DIM), F32)], axis=1)
    return cos_t, sin_t


def _rms_to_bf16(h_ref, g_ref):
    x = h_ref[...]
    ms = jnp.mean(x * x, axis=-1, keepdims=True)
    return (x * lax.rsqrt(ms + RMS_EPS) * g_ref[...]).astype(BF16)


def _norm_proj_kernel(h_ref, g_ref, w_ref, wg_ref, o_ref, og_ref, u_scr):
    @pl.when(pl.program_id(2) == 0)
    def _():
        u = _rms_to_bf16(h_ref, g_ref)
        u_scr[...] = u
        og_ref[...] = _dot(u, wg_ref[...])

    u = u_scr[...]
    tn = w_ref.shape[1]
    for c0 in range(0, tn, MXU_COLS):
        width = min(MXU_COLS, tn - c0)
        res = _dot(u, w_ref[:, c0:c0 + width])
        for c in range(width // HEAD_DIM):
            o_ref[c0 // HEAD_DIM + c] = res[:, c * HEAD_DIM:(c + 1) * HEAD_DIM].astype(o_ref.dtype)


def _norm_proj(h, g, w, wg, *, tm, tn):
    b, s, d = h.shape
    n = w.shape[1]
    ng = wg.shape[1]
    ncb = tn // HEAD_DIM
    return pl.pallas_call(
        _norm_proj_kernel,
        grid=(b, s // tm, n // tn),
        in_specs=[
            pl.BlockSpec((None, tm, d), lambda bi, si, j: (bi, si, 0)),
            pl.BlockSpec((1, d), lambda bi, si, j: (0, 0)),
            pl.BlockSpec((d, tn), lambda bi, si, j: (0, j)),
            pl.BlockSpec((d, ng), lambda bi, si, j: (0, 0)),
        ],
        out_specs=[
            pl.BlockSpec((None, ncb, tm, HEAD_DIM), lambda bi, si, j: (bi, j, si, 0)),
            pl.BlockSpec((None, tm, ng), lambda bi, si, j: (bi, si, 0)),
        ],
        out_shape=[
            jax.ShapeDtypeStruct((b, n // HEAD_DIM, s, HEAD_DIM), BF16),
            jax.ShapeDtypeStruct((b, s, ng), F32),
        ],
        scratch_shapes=[pltpu.VMEM((tm, d), BF16)],
        compiler_params=_cparams(("parallel", "parallel", "arbitrary")),
        name="norm_proj",
    )(h, g, w, wg)


def _ffn_up_kernel(u_ref, w_ref, cw_ref, cb_ref, o_ref, halo_scr):
    si = pl.program_id(1)
    j = pl.program_id(2)
    tm, tn = o_ref.shape

    @pl.when(si == 0)
    def _():
        halo_scr[j] = jnp.zeros(halo_scr.shape[1:], F32)

    u = u_ref[...]
    cw = cw_ref[...]
    cb = cb_ref[...]
    for c0 in range(0, tn, HEAD_DIM):
        cols = slice(c0, c0 + HEAD_DIM)
        res = _dot(u, w_ref[:, 2 * c0:2 * c0 + 2 * HEAD_DIM])
        gate = res[:, :HEAD_DIM]
        up = res[:, HEAD_DIM:]
        ext = jnp.concatenate([halo_scr[j, :, cols], gate], axis=0)
        halo_scr[j, :, cols] = gate[tm - 8:, :]
        conv = cb[:, cols] + cw[FFN_CONV - 1:FFN_CONV, cols] * gate
        for k in range(FFN_CONV - 1):
            off = 8 - (FFN_CONV - 1) + k
            conv = conv + cw[k:k + 1, cols] * ext[off:off + tm, :]
        o_ref[:, cols] = (conv * _sigmoid(conv) * up).astype(o_ref.dtype)


def _ffn_up(u, w_cat, cw, cb, *, tm, tn):
    b, s, d = u.shape
    f = w_cat.shape[1] // 2
    return pl.pallas_call(
        _ffn_up_kernel,
        grid=(b, s // tm, f // tn),
        in_specs=[
            pl.BlockSpec((None, tm, d), lambda bi, si, j: (bi, si, 0)),
            pl.BlockSpec((d, 2 * tn), lambda bi, si, j: (0, j)),
            pl.BlockSpec((FFN_CONV, tn), lambda bi, si, j: (0, j)),
            pl.BlockSpec((1, tn), lambda bi, si, j: (0, j)),
        ],
        out_specs=pl.BlockSpec((None, tm, tn), lambda bi, si, j: (bi, si, j)),
        out_shape=jax.ShapeDtypeStruct((b, s, f), BF16),
        scratch_shapes=[pltpu.VMEM((f // tn, 8, tn), F32)],
        compiler_params=_cparams(("parallel", "arbitrary", "arbitrary")),
        name="ffn_up",
    )(u, w_cat, cw, cb)


def _out_proj_norm_kernel(h_ref, a1_ref, w1_ref, a2_ref, w2_ref, g_ref, ho_ref, u_ref):
    tm, d = ho_ref.shape
    a1 = a1_ref[...]
    a2 = a2_ref[...]
    ss = jnp.zeros((tm, 1), F32)
    for c0 in range(0, d, OUT_COLS):
        cols = slice(c0, c0 + OUT_COLS)
        acc = h_ref[:, cols] + _dot(a1, w1_ref[:, cols]) + _dot(a2, w2_ref[:, cols])
        ho_ref[:, cols] = acc
        ss = ss + jnp.sum(acc * acc, axis=1, keepdims=True)
    inv = lax.rsqrt(ss * (1.0 / d) + RMS_EPS)
    for c0 in range(0, d, OUT_COLS):
        cols = slice(c0, c0 + OUT_COLS)
        u_ref[:, cols] = (ho_ref[:, cols] * inv * g_ref[:, cols]).astype(u_ref.dtype)


def _out_proj_norm(h, a1, w1, a2, w2, g, *, tm):
    b, s, d = h.shape
    k1 = a1.shape[-1]
    k2 = a2.shape[-1]
    rows = lambda k: pl.BlockSpec((None, tm, k), lambda bi, si: (bi, si, 0))
    whole = lambda k: pl.BlockSpec((k, d), lambda bi, si: (0, 0))
    return pl.pallas_call(
        _out_proj_norm_kernel,
        grid=(b, s // tm),
        in_specs=[rows(d), rows(k1), whole(k1), rows(k2), whole(k2), pl.BlockSpec((1, d), lambda bi, si: (0, 0))],
        out_specs=[rows(d), rows(d)],
        out_shape=[jax.ShapeDtypeStruct((b, s, d), F32), jax.ShapeDtypeStruct((b, s, d), BF16)],
        compiler_params=_cparams(("parallel", "parallel")),
        name="out_proj_norm",
    )(h, a1, w1, a2, w2, g)


def _proj_res_kernel(*refs):
    h_ref, o_ref = refs[0], refs[-1]
    ops = refs[1:-1]
    acc = h_ref[...]
    for i in range(0, len(ops), 2):
        acc = acc + _dot(ops[i][...], ops[i + 1][...])
    o_ref[...] = acc


def _proj_res(h, pairs, *, tm, tn):
    b, s, d = h.shape
    in_specs = [pl.BlockSpec((None, tm, tn), lambda bi, si, j: (bi, si, j))]
    args = [h]
    for a, w in pairs:
        k = a.shape[-1]
        in_specs.append(pl.BlockSpec((None, tm, k), lambda bi, si, j: (bi, si, 0)))
        in_specs.append(pl.BlockSpec((k, tn), lambda bi, si, j: (0, j)))
        args += [a, w]
    return pl.pallas_call(
        _proj_res_kernel,
        grid=(b, s // tm, d // tn),
        in_specs=in_specs,
        out_specs=pl.BlockSpec((None, tm, tn), lambda bi, si, j: (bi, si, j)),
        out_shape=jax.ShapeDtypeStruct((b, s, d), F32),
        compiler_params=_cparams(("parallel", "parallel", "arbitrary")),
        name="proj_res",
    )(*args)


def _rmsnorm_kernel(h_ref, g_ref, o_ref):
    x = h_ref[...]
    ms = jnp.mean(x * x, axis=-1, keepdims=True)
    o_ref[...] = x * lax.rsqrt(ms + RMS_EPS) * g_ref[...]


def _rmsnorm(h, g, *, tm):
    b, s, d = h.shape
    return pl.pallas_call(
        _rmsnorm_kernel,
        grid=(b, s // tm),
        in_specs=[pl.BlockSpec((None, tm, d), lambda bi, si: (bi, si, 0)),
                  pl.BlockSpec((1, d), lambda bi, si: (0, 0))],
        out_specs=pl.BlockSpec((None, tm, d), lambda bi, si: (bi, si, 0)),
        out_shape=jax.ShapeDtypeStruct((b, s, d), F32),
        compiler_params=_cparams(("parallel", "parallel")),
        name="final_norm",
    )(h, g)


def _mlstm_kernel(q_ref, k_ref, v_ref, og_ref, ig_ref, fg_ref, bi_ref, bf_ref,
                  cwq_ref, cwk_ref, cbq_ref, cbk_ref, hn_ref, out_ref,
                  pad_scr, qc_scr, kc_scr, c_scr, b_scr, d_scr):
    nh, s, _ = q_ref.shape
    nch = s // CHUNK

    def conv_silu(src_ref, cw_ref, cb_ref, dst_ref, hh, scale):
        lanes = slice(hh * HEAD_DIM, (hh + 1) * HEAD_DIM)
        pad_scr[0:8, :] = jnp.zeros((8, HEAD_DIM), F32)
        for p in range(nch):
            pad_scr[8 + p * CHUNK:8 + (p + 1) * CHUNK, :] = src_ref[hh, p * CHUNK:(p + 1) * CHUNK, :].astype(F32)
        cw = cw_ref[:, lanes]
        for p in range(nch):
            acc = cb_ref[:, lanes] + cw[0:1, :] * pad_scr[pl.ds(8 - (A_CONV - 1) + p * CHUNK, CHUNK), :]
            for j in range(1, A_CONV):
                acc = acc + cw[j:j + 1, :] * pad_scr[pl.ds(8 - (A_CONV - 1) + j + p * CHUNK, CHUNK), :]
            y = acc * _sigmoid(acc)
            if scale != 1.0:
                y = y * scale
            dst_ref[hh, p * CHUNK:(p + 1) * CHUNK, :] = y.astype(BF16)

    r_i = _iota((CHUNK, CHUNK), 0)
    c_i = _iota((CHUNK, CHUNK), 1)
    upper = (r_i <= c_i).astype(BF16)
    for hh in range(nh):
        conv_silu(q_ref, cwq_ref, cbq_ref, qc_scr, hh, 1.0)
        conv_silu(k_ref, cwk_ref, cbk_ref, kc_scr, hh, SCALE)
        log_f = _log_sigmoid(fg_ref[hh] + bf_ref[hh])
        log_i = ig_ref[hh] + bi_ref[hh]
        b_all = _dot_split(log_f, upper, 3)
        b_scr[hh] = b_all
        d_scr[hh] = log_i - b_all
    c_scr[...] = jnp.zeros(c_scr.shape, F32)
    causal = c_i <= r_i
    ones_blk = jnp.ones((CHUNK, HEAD_DIM), BF16)

    def head_chunk(hh, c, t0, m):
        lanes = slice(hh * HEAD_DIM, (hh + 1) * HEAD_DIM)
        q = qc_scr[hh, pl.ds(t0, CHUNK), :]
        k = kc_scr[hh, pl.ds(t0, CHUNK), :]
        v_aug = jnp.concatenate([v_ref[hh, pl.ds(t0, CHUNK), :], ones_blk], axis=1)
        b_row = b_scr[hh, pl.ds(c, 1), :]
        d_row = d_scr[hh, pl.ds(c, 1), :]
        b_col = jnp.transpose(jnp.broadcast_to(b_row, (CHUNK, CHUNK)))
        d_col = jnp.transpose(jnp.broadcast_to(d_row, (CHUNK, CHUNK)))
        dmat = jnp.where(causal, b_col + d_row, -jnp.inf)
        g = b_col + m
        m_row = jnp.maximum(g, jnp.max(dmat, axis=1, keepdims=True))
        w = jnp.exp(dmat - m_row) * _dot_nt(q, k)
        w_state = jnp.exp(g - m_row)
        c_old = c_scr[hh]
        inter = _dot(q, c_old.astype(BF16))
        intra = _dot(w.astype(BF16), v_aug)
        tot = jnp.concatenate([w_state, w_state], axis=1) * inter + intra
        num = tot[:, :HEAD_DIM]
        den = tot[:, HEAD_DIM:]
        hv = num / jnp.maximum(jnp.abs(den), jnp.exp(-m_row))
        y = hv * lax.rsqrt(jnp.mean(hv * hv, axis=-1, keepdims=True) + RMS_EPS) * hn_ref[:, lanes]
        y = y * _sigmoid(og_ref[hh, pl.ds(t0, CHUNK), :].astype(F32))
        out_ref[pl.ds(t0, CHUNK), lanes] = y.astype(out_ref.dtype)
        b_last = b_col[CHUNK - 1:CHUNK, :]
        w_new = b_last + d_col
        m_new = jnp.maximum(b_last + m, jnp.max(w_new, axis=0, keepdims=True))
        decay = jnp.exp(b_last + m - m_new)
        kw = jnp.exp(w_new - m_new) * k.astype(F32)
        kw_t = jnp.transpose(kw).astype(BF16)
        c_scr[hh] = jnp.concatenate([decay, decay], axis=1) * c_old + _dot(kw_t, v_aug)
        return m_new

    def body(c, ms):
        t0 = pl.multiple_of(c * CHUNK, CHUNK)
        return tuple(head_chunk(hh, c, t0, ms[hh]) for hh in range(nh))

    lax.fori_loop(0, nch, body, tuple(jnp.zeros((1, CHUNK), F32) for _ in range(nh)))


def _mlstm(p, gi, gf, bias_i, bias_f, cw, cb, hn):
    b, _, s, _ = p.shape
    nch = s // CHUNK
    nh = HEADS_PER_STEP
    nblk = N_HEADS // nh
    w = nh * HEAD_DIM
    slab = lambda off: pl.BlockSpec((None, nh, s, HEAD_DIM), lambda bi, hi: (bi, off + hi, 0, 0))
    gate = pl.BlockSpec((None, nh, nch, CHUNK), lambda bi, hi: (bi, hi, 0, 0))
    hb = pl.BlockSpec((nh, 1, CHUNK), lambda bi, hi: (hi, 0, 0))
    return pl.pallas_call(
        _mlstm_kernel,
        grid=(b, nblk),
        in_specs=[
            slab(0), slab(nblk), slab(2 * nblk), slab(3 * nblk), gate, gate, hb, hb,
            pl.BlockSpec((A_CONV, w), lambda bi, hi: (0, hi)),
            pl.BlockSpec((A_CONV, w), lambda bi, hi: (0, nblk + hi)),
            pl.BlockSpec((1, w), lambda bi, hi: (0, hi)),
            pl.BlockSpec((1, w), lambda bi, hi: (0, nblk + hi)),
            pl.BlockSpec((1, w), lambda bi, hi: (0, hi)),
        ],
        out_specs=pl.BlockSpec((None, s, w), lambda bi, hi: (bi, 0, hi)),
        out_shape=jax.ShapeDtypeStruct((b, s, N_HEADS * HEAD_DIM), BF16),
        scratch_shapes=[
            pltpu.VMEM((s + 8, HEAD_DIM), F32),
            pltpu.VMEM((nh, s, HEAD_DIM), BF16),
            pltpu.VMEM((nh, s, HEAD_DIM), BF16),
            pltpu.VMEM((nh, HEAD_DIM, 2 * HEAD_DIM), F32),
            pltpu.VMEM((nh, nch, CHUNK), F32),
            pltpu.VMEM((nh, nch, CHUNK), F32),
        ],
        compiler_params=_cparams(("parallel", "parallel")),
        name="mlstm",
    )(p, p, p, p, gi, gf, bias_i, bias_f, cw, cw, cb, cb, hn)


def _sb_kernel(q_ref, k_ref, v_ref, o_ref):
    qi = pl.program_id(2)
    n_rg = SB_ROW_GROUPS
    sub = q_ref.shape[0] // n_rg
    r_i = _iota((sub, sub), 0)
    c_i = _iota((sub, sub), 1)
    suffix = (r_i > c_i).astype(BF16)
    strict = c_i < r_i
    qs = [q_ref[g * sub:(g + 1) * sub, :] for g in range(n_rg)]
    j_diag = [n_rg * qi + g for g in range(n_rg)]

    def sub_block(q, j, run, diag):
        k0 = pl.multiple_of(j * sub, sub)
        k = k_ref[pl.ds(k0, sub), :]
        v = v_ref[pl.ds(k0, sub), :]
        z2 = _dot_nt(q, k)
        sp2 = jnp.maximum(z2, 0.0) + jnp.log2(1.0 + jnp.exp2(-jnp.abs(z2)))
        log_keep = -sp2
        if diag:
            log_keep = jnp.where(strict, log_keep, 0.0)
        later = _dot(log_keep.astype(BF16), suffix) + run
        a = jnp.exp2(z2 - sp2 + later)
        if diag:
            a = jnp.where(strict, a, 0.0)
        return _dot(a.astype(BF16), v), jnp.sum(log_keep, axis=1, keepdims=True)

    def any_alive(runs):
        top = runs[0]
        for run in runs[1:]:
            top = jnp.maximum(top, run)
        return (jnp.max(top) > SB_DEAD_LOG2).astype(jnp.int32)

    accs, runs = [], []
    for g in range(n_rg):
        o, sm = sub_block(qs[g], j_diag[g], 0.0, True)
        accs.append(o)
        runs.append(sm)

    def cond(carry):
        n, alive = carry[0], carry[1]
        return (n <= j_diag[-1]) & (alive > 0)

    def body(carry):
        n = carry[0]
        accs, runs = list(carry[2]), list(carry[3])
        for g in range(n_rg):
            j = j_diag[g] - n
            o, sm = sub_block(qs[g], jnp.maximum(j, 0), runs[g], False)
            live = j >= 0
            accs[g] = accs[g] + jnp.where(live, o, 0.0)
            runs[g] = runs[g] + jnp.where(live, sm, 0.0)
        return n + 1, any_alive(runs), tuple(accs), tuple(runs)

    out = lax.while_loop(cond, body, (jnp.int32(1), any_alive(runs), tuple(accs), tuple(runs)))
    for g in range(n_rg):
        o_ref[g * sub:(g + 1) * sub, :] = out[2][g].astype(o_ref.dtype)


def _stick_breaking(p, *, q_off, k_off, v_off, tq):
    b, _, s, _ = p.shape
    return pl.pallas_call(
        _sb_kernel,
        grid=(b, N_HEADS, s // tq),
        in_specs=[
            pl.BlockSpec((None, None, tq, HEAD_DIM), lambda bi, hi, qi: (bi, q_off + hi, qi, 0)),
            pl.BlockSpec((None, None, s, HEAD_DIM), lambda bi, hi, qi: (bi, k_off + hi, 0, 0)),
            pl.BlockSpec((None, None, s, HEAD_DIM), lambda bi, hi, qi: (bi, v_off + hi, 0, 0)),
        ],
        out_specs=pl.BlockSpec((None, tq, HEAD_DIM), lambda bi, hi, qi: (bi, qi, hi)),
        out_shape=jax.ShapeDtypeStruct((b, s, N_HEADS * HEAD_DIM), BF16),
        compiler_params=_cparams(("parallel", "parallel", "arbitrary")),
        name="stick_breaking",
    )(p, p, p)


def _gelu_tanh(x):
    return 0.5 * x * (1.0 + jnp.tanh(np.sqrt(2.0 / np.pi).astype(np.float32) * (x + 0.044715 * (x * x * x))))


def _nsa_prep_kernel(kcs_ref, vcs_ref, ks_ref, kw_ref, pos_ref, w1_ref, w2_ref,
                     cos_ref, sin_ref, cosc_ref, sinc_ref,
                     kc_out, vc_out, ks_out, kw_out, f32_scr):
    s = ks_ref.shape[0]
    ng = s // CMP_STRIDE

    def compress(src_ref, idx):
        f32_scr[...] = src_ref[...].astype(F32)
        xa = jnp.zeros((ng, HEAD_DIM), F32)
        xb = jnp.zeros((ng, HEAD_DIM), F32)
        for l in range(CMP_STRIDE):
            rows = f32_scr[pl.ds(l, ng, stride=CMP_STRIDE), :]
            xa = xa + _dot((rows + pos_ref[idx, l:l + 1, :]).astype(BF16), w1_ref[idx, l])
            xb = xb + _dot((rows + pos_ref[idx, CMP_STRIDE + l:CMP_STRIDE + l + 1, :]).astype(BF16),
                           w1_ref[idx, CMP_STRIDE + l])
        hid = _gelu_tanh(xa + pltpu.roll(xb, ng - 1, axis=0))
        return _dot(hid.astype(BF16), w2_ref[idx])

    kc = compress(kcs_ref, 0)
    kc_out[...] = (kc * cosc_ref[...] + _rope_swap(kc) * sinc_ref[...]).astype(kc_out.dtype)
    vc_out[...] = compress(vcs_ref, 1).astype(vc_out.dtype)
    for src, dst in ((ks_ref, ks_out), (kw_ref, kw_out)):
        x = src[...].astype(F32)
        dst[...] = (x * cos_ref[...] + _rope_swap(x) * sin_ref[...]).astype(dst.dtype)


def _nsa_prep(p, pos, w1, w2, cos, sin, cosc, sinc, *, kc_off, vc_off, ks_off, kw_off):
    b, _, s, _ = p.shape
    ng = s // CMP_STRIDE
    slab = lambda off: pl.BlockSpec((None, None, s, HEAD_DIM), lambda bi, gi: (bi, off + gi, 0, 0))
    full = lambda a: pl.BlockSpec(a.shape, lambda bi, gi: (0,) * a.ndim)
    out_c = pl.BlockSpec((None, None, ng, HEAD_DIM), lambda bi, gi: (bi, gi, 0, 0))
    out_s = pl.BlockSpec((None, None, s, HEAD_DIM), lambda bi, gi: (bi, gi, 0, 0))
    return pl.pallas_call(
        _nsa_prep_kernel,
        grid=(b, KV_GROUPS),
        in_specs=[slab(kc_off), slab(vc_off), slab(ks_off), slab(kw_off),
                  full(pos), full(w1), full(w2), full(cos), full(sin), full(cosc), full(sinc)],
        out_specs=[out_c, out_c, out_s, out_s],
        out_shape=[jax.ShapeDtypeStruct((b, KV_GROUPS, ng, HEAD_DIM), BF16),
                   jax.ShapeDtypeStruct((b, KV_GROUPS, ng, HEAD_DIM), BF16),
                   jax.ShapeDtypeStruct((b, KV_GROUPS, s, HEAD_DIM), BF16),
                   jax.ShapeDtypeStruct((b, KV_GROUPS, s, HEAD_DIM), BF16)],
        scratch_shapes=[pltpu.VMEM((s, HEAD_DIM), F32)],
        compiler_params=_cparams(("parallel", "parallel")),
        name="nsa_prep",
    )(p, p, p, p, pos, w1, w2, cos, sin, cosc, sinc)


def _masked_softmax(sc, mask):
    sc = jnp.where(mask, sc, -jnp.inf)
    m = jnp.max(sc, axis=-1, keepdims=True)
    m = jnp.where(m > -jnp.inf, m, 0.0)
    e = jnp.exp(sc - m)
    return e / jnp.maximum(jnp.sum(e, axis=-1, keepdims=True), 1e-30)


def _nsa_kernel(q_ref, kc_ref, vc_ref, ks_ref, vs_ref, kw_ref, vw_ref, gate_ref, cos_ref, sin_ref,
                cover_t_ref, o_ref, *, tk):
    qi = pl.program_id(2)
    r = q_ref.shape[0]
    tq = q_ref.shape[1]
    s_len = ks_ref.shape[0]
    n_cmp = (s_len - CMP_LEN) // CMP_STRIDE + 1
    t0 = qi * tq

    qf = q_ref[...].astype(F32).reshape(r * tq, HEAD_DIM)
    cos = jnp.concatenate([cos_ref[...]] * r, axis=0)
    sin = jnp.concatenate([sin_ref[...]] * r, axis=0)
    qb = (qf * cos + _rope_swap(qf) * sin).astype(BF16)

    t_col = t0 + _iota((tq, HEAD_DIM), 0)
    lane = _iota((tq, HEAD_DIM), 1)

    ncl = kc_ref.shape[0]
    sc = _dot_nt(qb, kc_ref[...]).reshape(r, tq, ncl)
    cmp_ok = ((lane * CMP_STRIDE + (CMP_LEN - 1) <= t_col) & (lane < n_cmp))[:, :ncl]
    p_cmp = _masked_softmax(sc, cmp_ok[None])
    o_cmp = _dot(p_cmp.reshape(r * tq, ncl).astype(BF16), vc_ref[...]).reshape(r, tq, HEAD_DIM)

    band = WINDOW + tq
    w0 = pl.multiple_of(jnp.maximum(t0 - WINDOW, 0), tq)
    k_win = kw_ref[pl.ds(w0, band), :]
    v_win = vw_ref[pl.ds(w0, band), :]
    diff = (t0 + _iota((tq, band), 0)) - (w0 + _iota((tq, band), 1))
    win_ok = (diff >= 0) & (diff < WINDOW)
    sw = _dot_nt(qb, k_win).reshape(r, tq, band)
    p_win = _masked_softmax(sw, win_ok[None])
    o_win = _dot(p_win.reshape(r * tq, band).astype(BF16), v_win).reshape(r, tq, HEAD_DIM)

    p_sum = p_cmp[0]
    for h in range(1, r):
        p_sum = p_sum + p_cmp[h]
    n_blk = s_len // SEL_LEN
    p_slc_t = _dot_nt_split(cover_t_ref[...], p_sum, 3)[:n_blk, :]
    blk = _iota((n_blk, tq), 0)
    blk_t = (t0 + _iota((n_blk, tq), 1)) >> SEL_SHIFT
    started = blk <= blk_t
    forced = (blk == 0) | (started & (blk_t - blk < N_LOCAL))
    score = jnp.where(started, p_slc_t + jnp.where(forced, FORCE_BONUS, 0.0), -jnp.inf)
    rank = jnp.zeros((n_blk, tq), F32)
    for jp in range(n_blk):
        row = score[jp:jp + 1, :]
        beats = (row > score) | ((row == score) & (blk > jp))
        rank = rank + jnp.where(beats, 1.0, 0.0)
    sel_t = jnp.where((rank < N_SEL) & started, 1.0, 0.0)
    sel_t = jnp.concatenate([sel_t, jnp.zeros((HEAD_DIM - n_blk, tq), F32)], axis=0)
    sel = jnp.transpose(sel_t).astype(BF16)

    e_row = _iota((HEAD_DIM, tk), 0)
    e_blk = _iota((HEAD_DIM, tk), 1) >> SEL_SHIFT
    s_loc = _iota((tq, tk), 1)
    t_q = t0 + _iota((tq, tk), 0)
    n_chain = r
    hc = r // n_chain
    q_chain = [qb[c * hc * tq:(c + 1) * hc * tq, :] for c in range(n_chain)]
    ones_blk = jnp.ones((tk, HEAD_DIM), BF16)

    def sel_tile(kt, carry):
        k0 = pl.multiple_of(kt * tk, tk)
        k = ks_ref[pl.ds(k0, tk), :]
        v_aug = jnp.concatenate([vs_ref[pl.ds(k0, tk), :], ones_blk], axis=1)
        expand = (e_row == e_blk + kt * (tk // SEL_LEN)).astype(BF16)
        ok = (_dot(sel, expand) > 0.5) & (k0 + s_loc <= t_q)
        bias = jnp.where(ok, 0.0, -jnp.inf)[None]
        new = []
        for c in range(n_chain):
            m, acc = carry[c]
            sc_t = _dot_nt(q_chain[c], k).reshape(hc, tq, tk) + bias
            m_new = jnp.maximum(m, jnp.max(sc_t, axis=-1, keepdims=True))
            alpha = jnp.exp(m - m_new)
            e = jnp.exp(sc_t - m_new)
            pv = _dot(e.reshape(hc * tq, tk).astype(BF16), v_aug).reshape(hc, tq, 2 * HEAD_DIM)
            new.append((m_new, alpha * acc + pv))
        return tuple(new)

    n_tiles = (t0 + tq + tk - 1) // tk
    init = tuple((jnp.full((hc, tq, 1), -jnp.inf, F32), jnp.zeros((hc, tq, 2 * HEAD_DIM), F32))
                 for _ in range(n_chain))
    chains = lax.fori_loop(0, n_tiles, sel_tile, init)

    gates = _sigmoid(gate_ref[...])
    for h in range(r):
        acc_s = chains[h // hc][1][h % hc]
        o_slc = acc_s[:, :HEAD_DIM] / jnp.maximum(acc_s[:, HEAD_DIM:], 1e-30)
        out = (gates[:, 3 * h:3 * h + 1] * o_cmp[h] + gates[:, 3 * h + 1:3 * h + 2] * o_slc
               + gates[:, 3 * h + 2:3 * h + 3] * o_win[h])
        o_ref[:, h * HEAD_DIM:(h + 1) * HEAD_DIM] = out.astype(o_ref.dtype)


def _nsa(p, kc, vc, ks, kw, gates, cos, sin, cover, *, vs_off, vw_off, tq, tk):
    b, _, s, _ = p.shape
    r = GROUP_HEADS
    ng = kc.shape[2]
    kv = lambda: pl.BlockSpec((None, None, s, HEAD_DIM), lambda bi, gi, qi: (bi, gi, 0, 0))
    pslab = lambda off: pl.BlockSpec((None, None, s, HEAD_DIM), lambda bi, gi, qi: (bi, off + gi, 0, 0))
    cmp_spec = pl.BlockSpec((None, None, ng, HEAD_DIM), lambda bi, gi, qi: (bi, gi, 0, 0))
    return pl.pallas_call(
        functools.partial(_nsa_kernel, tk=tk),
        grid=(b, KV_GROUPS, s // tq),
        in_specs=[
            pl.BlockSpec((None, r, tq, HEAD_DIM), lambda bi, gi, qi: (bi, gi, qi, 0)),
            cmp_spec, cmp_spec, kv(), pslab(vs_off), kv(), pslab(vw_off),
            pl.BlockSpec((None, tq, HEAD_DIM), lambda bi, gi, qi: (bi, qi, gi)),
            pl.BlockSpec((tq, HEAD_DIM), lambda bi, gi, qi: (qi, 0)),
            pl.BlockSpec((tq, HEAD_DIM), lambda bi, gi, qi: (qi, 0)),
            pl.BlockSpec(cover.shape, lambda bi, gi, qi: (0, 0)),
        ],
        out_specs=pl.BlockSpec((None, tq, r * HEAD_DIM), lambda bi, gi, qi: (bi, qi, gi)),
        out_shape=jax.ShapeDtypeStruct((b, s, N_HEADS * HEAD_DIM), BF16),
        compiler_params=_cparams(("parallel", "parallel", "arbitrary")),
        name="nsa",
    )(p, kc, vc, ks, p, kw, p, gates, cos, sin, cover)


def _hgrn_tables():
    r = np.arange(CHUNK)[:, None]
    j = np.arange(CHUNK)[None, :]
    mats = []
    for sz in HGRN_LEVELS:
        mid = (r // (2 * sz)) * (2 * sz) + sz
        right = (r // sz) % 2 == 1
        mats.append(np.where(right, (j >= mid) & (j <= r), (j > r) & (j <= mid - 1)))
    mats.append(j <= r)
    mats.append(j > r)
    return np.concatenate(mats, axis=0).astype(np.float32)


def _hgrn_kernel(q_ref, f_ref, i_ref, g_ref, lb_ref, hn_ref, tab_ref, out_ref, st_scr):
    nh, s, _ = q_ref.shape
    nch = s // CHUNK
    nlev = len(HGRN_LEVELS)
    r_i = _iota((CHUNK, CHUNK), 0)
    c_i = _iota((CHUNK, CHUNK), 1)
    row = _iota((CHUNK, HEAD_DIM), 0)
    st_scr[...] = jnp.zeros(st_scr.shape, F32)

    def head_chunk(hh, t0):
        lanes = slice(hh * HEAD_DIM, (hh + 1) * HEAD_DIM)
        lb = lb_ref[:, lanes]
        q = q_ref[hh, pl.ds(t0, CHUNK), :].astype(F32)
        fp = f_ref[hh, pl.ds(t0, CHUNK), :].astype(F32)
        v = i_ref[hh, pl.ds(t0, CHUNK), :]
        log_lb = jnp.log(lb)
        a2 = jnp.log1p(-lb) - _softplus(-fp)
        hi = jnp.maximum(log_lb, a2)
        lo = jnp.minimum(log_lb, a2)
        log_f = hi + jnp.log(1.0 + jnp.exp(lo - hi))
        k = (1.0 - lb) * _sigmoid(-fp)
        x_all = _dot_split_rhs(tab_ref[...], log_f, 2)
        att = jnp.where(r_i == c_i, jnp.sum(q * k, axis=1, keepdims=True), 0.0)
        for lv, sz in enumerate(HGRN_LEVELS):
            sh = sz.bit_length() - 1
            e = jnp.exp(x_all[lv * CHUNK:(lv + 1) * CHUNK, :])
            right = ((row >> sh) & 1) == 1
            qt = jnp.where(right, q * e, 0.0).astype(BF16)
            kt = jnp.where(right, 0.0, k * e).astype(BF16)
            same = (r_i >> (sh + 1)) == (c_i >> (sh + 1))
            att = att + jnp.where(same, _dot_nt(qt, kt), 0.0)
        bc = x_all[nlev * CHUNK:(nlev + 1) * CHUNK, :]
        rev = x_all[(nlev + 1) * CHUNK:(nlev + 2) * CHUNK, :]
        st = st_scr[hh]
        o = _dot(att.astype(BF16), v) + _dot_nt((q * jnp.exp(bc)).astype(BF16), st.astype(BF16))
        y = o * lax.rsqrt(jnp.mean(o * o, axis=-1, keepdims=True) + RMS_EPS) * hn_ref[:, lanes]
        gg = g_ref[hh, pl.ds(t0, CHUNK), :].astype(F32)
        out_ref[pl.ds(t0, CHUNK), lanes] = (y * (gg * _sigmoid(gg))).astype(out_ref.dtype)
        v_t = jnp.transpose(v.astype(F32)).astype(BF16)
        st_scr[hh] = st * jnp.exp(bc[CHUNK - 1:CHUNK, :]) + _dot(v_t, (k * jnp.exp(rev)).astype(BF16))

    def body(c, carry):
        t0 = pl.multiple_of(c * CHUNK, CHUNK)
        for hh in range(nh):
            head_chunk(hh, t0)
        return carry

    lax.fori_loop(0, nch, body, 0)


def _hgrn(p, lb, hn, tab, *, q_off, f_off, i_off, g_off):
    b, _, s, _ = p.shape
    nh = HEADS_PER_STEP
    nblk = N_HEADS // nh
    w = nh * HEAD_DIM
    slab = lambda off: pl.BlockSpec((None, nh, s, HEAD_DIM), lambda bi, hi: (bi, off // nh + hi, 0, 0))
    return pl.pallas_call(
        _hgrn_kernel,
        grid=(b, nblk),
        in_specs=[slab(q_off), slab(f_off), slab(i_off), slab(g_off),
                  pl.BlockSpec((1, w), lambda bi, hi: (0, hi)),
                  pl.BlockSpec((1, w), lambda bi, hi: (0, hi)),
                  pl.BlockSpec(tab.shape, lambda bi, hi: (0, 0))],
        out_specs=pl.BlockSpec((None, s, w), lambda bi, hi: (bi, 0, hi)),
        out_shape=jax.ShapeDtypeStruct((b, s, N_HEADS * HEAD_DIM), BF16),
        scratch_shapes=[pltpu.VMEM((nh, HEAD_DIM, HEAD_DIM), F32)],
        compiler_params=_cparams(("parallel", "parallel")),
        name="hgrn2",
    )(p, p, p, p, lb, hn, tab)


def _pad_cols(w, n):
    return jnp.pad(w, ((0, 0), (0, n - w.shape[1])))


def _row_tile(s):
    return min(1024, s)


def _layer_ab(h, norm_g, w_in, conv_w, conv_b, gate_b, head_norm, w_out, ffn_norm_g):
    b, s, d = h.shape
    aw = N_HEADS * HEAD_DIM
    tm = _row_tile(s)
    g0 = 4 * aw
    g1 = g0 + 2 * N_HEADS
    w_main = jnp.concatenate([w_in[:, :g0], w_in[:, g1:g1 + aw] * (SCALE * LOG2_E), w_in[:, g1 + aw:]],
                             axis=1).astype(BF16)
    w_gate = _pad_cols(w_in[:, g0:g1], HEAD_DIM).astype(BF16)
    p, gates = _norm_proj(h, norm_g.reshape(1, d), w_main, w_gate, tm=tm, tn=w_main.shape[1] // PROJ_COL_TILES)
    nch = s // CHUNK
    g_t = jnp.transpose(gates[:, :, :2 * N_HEADS], (0, 2, 1))
    gi = g_t[:, :N_HEADS].reshape(b, N_HEADS, nch, CHUNK)
    gf = g_t[:, N_HEADS:].reshape(b, N_HEADS, nch, CHUNK)
    bias_i = jnp.broadcast_to(gate_b[:N_HEADS, None, None], (N_HEADS, 1, CHUNK)).astype(F32)
    bias_f = jnp.broadcast_to(gate_b[N_HEADS:, None, None], (N_HEADS, 1, CHUNK)).astype(F32)
    h_a = _mlstm(p, gi, gf, bias_i, bias_f, conv_w, conv_b.reshape(1, -1), head_norm.reshape(1, -1))
    h_b = _stick_breaking(p, q_off=4 * N_HEADS, k_off=5 * N_HEADS, v_off=6 * N_HEADS, tq=min(SB_ROW_GROUPS * SB_SUB, s))
    w_o = w_out.astype(BF16)
    return _out_proj_norm(h, h_a, w_o[:aw], h_b, w_o[aw:], ffn_norm_g.reshape(1, d), tm=min(OUT_ROW_TILE, s))


def _layer_cd(h, norm_g, w_in, cmp_pos, cmp_w1, cmp_w2, lower_bound, head_norm, w_out, ffn_norm_g):
    b, s, d = h.shape
    cw = N_HEADS * HEAD_DIM
    kvw = KV_GROUPS * HEAD_DIM
    tm = _row_tile(s)
    n_gate = 3 * N_HEADS
    g0 = cw + 6 * kvw
    w_main = jnp.concatenate([w_in[:, :cw] * SCALE, w_in[:, cw:g0], w_in[:, g0 + n_gate:]], axis=1).astype(BF16)
    per_group = 3 * GROUP_HEADS
    w_gate = jnp.concatenate(
        [_pad_cols(w_in[:, g0 + g * per_group:g0 + (g + 1) * per_group], HEAD_DIM) for g in range(KV_GROUPS)],
        axis=1).astype(BF16)
    p, gates = _norm_proj(h, norm_g.reshape(1, d), w_main, w_gate, tm=tm, tn=w_main.shape[1] // PROJ_COL_TILES)
    pos = cmp_pos.astype(F32)
    w1 = cmp_w1.astype(BF16)
    w2 = cmp_w2.astype(BF16)
    cos, sin = _rope_tables(np.arange(s))
    ng = s // CMP_STRIDE
    cosc, sinc = _rope_tables(np.arange(ng) * CMP_STRIDE + CMP_LEN - 1)
    kc, vc, ks, kw = _nsa_prep(p, pos, w1, w2, cos, sin, cosc, sinc,
                               kc_off=8, vc_off=10, ks_off=12, kw_off=16)
    n_cmp = (s - CMP_LEN) // CMP_STRIDE + 1
    n_sel = s // SEL_LEN
    cs = np.arange(n_cmp) * CMP_STRIDE
    ss = np.arange(n_sel) * SEL_LEN
    cover_t = np.zeros((HEAD_DIM, ng), np.float32)
    cover_t[:n_sel, :n_cmp] = (cs[None, :] < ss[:, None] + SEL_LEN) & (cs[None, :] + CMP_LEN > ss[:, None])
    o_c = _nsa(p, kc, vc, ks, kw, gates, cos, sin, jnp.asarray(cover_t, BF16),
               vs_off=14, vw_off=18, tq=256, tk=min(512, s))
    o_d = _hgrn(p, lower_bound.reshape(1, -1).astype(F32), head_norm.reshape(1, -1),
                jnp.asarray(_hgrn_tables(), BF16), q_off=20, f_off=28, i_off=36, g_off=44)
    w_o = w_out.astype(BF16)
    return _out_proj_norm(h, o_c, w_o[:cw], o_d, w_o[cw:], ffn_norm_g.reshape(1, d), tm=min(OUT_ROW_TILE, s))


def _ffn(h, u, w_up, conv_w, conv_b, w_down):
    b, s, d = h.shape
    f = w_down.shape[0]
    tm = _row_tile(s)
    w_cat = w_up.astype(BF16).reshape(d, 2, f // HEAD_DIM, HEAD_DIM).transpose(0, 2, 1, 3).reshape(d, 2 * f)
    act = _ffn_up(u, w_cat, conv_w, conv_b.reshape(1, f), tm=tm, tn=f // FFN_COL_TILES)
    return _proj_res(h, [(act, w_down.astype(BF16))], tm=tm, tn=512)


def kernel(x, norm_mix, norm_ffn, norm_final, ab_w_in, ab_conv_w, ab_conv_b, ab_gate_b, ab_head_norm, ab_w_out, cd_w_in, cd_cmp_pos, cd_cmp_w1, cd_cmp_w2, hgrn_gamma, cd_head_norm, cd_w_out, ffn_w_up, ffn_conv_w, ffn_conv_b, ffn_w_down):
    depth = norm_mix.shape[0]
    sm = jax.nn.softmax(hgrn_gamma.astype(F32), axis=0)
    lb_all = jnp.cumsum(sm, axis=0) - sm[0]
    h = x
    for layer in range(depth):
        j = layer // 2
        if layer % 2 == 0:
            h, u = _layer_ab(h, norm_mix[layer], ab_w_in[j], ab_conv_w[j], ab_conv_b[j], ab_gate_b[j],
                             ab_head_norm[j], ab_w_out[j], norm_ffn[layer])
        else:
            h, u = _layer_cd(h, norm_mix[layer], cd_w_in[j], cd_cmp_pos[j], cd_cmp_w1[j], cd_cmp_w2[j],
                             lb_all[layer], cd_head_norm[j], cd_w_out[j], norm_ffn[layer])
        h = _ffn(h, u, ffn_w_up[layer], ffn_conv_w[layer], ffn_conv_b[layer], ffn_w_down[layer])
    return _rmsnorm(h, norm_final.reshape(1, -1), tm=min(512, h.shape[1]))
```

```python
import functools

import numpy as np
import jax
import jax.numpy as jnp
from jax import lax
from jax.experimental import pallas as pl
from jax.experimental.pallas import tpu as pltpu

F32 = jnp.float32
BF16 = jnp.bfloat16

HEAD_DIM = 128
N_HEADS = 8
KV_GROUPS = 2
GROUP_HEADS = N_HEADS // KV_GROUPS
A_CONV = 4
CMP_LEN = 32
CMP_STRIDE = 16
SEL_LEN = 64
SEL_SHIFT = 6
N_SEL = 8
N_LOCAL = 2
FORCE_BONUS = 1000.0
WINDOW = 256
FFN_CONV = 3
ROPE_THETA = 500000.0
ROPE_DIM = HEAD_DIM // 4
RMS_EPS = 1e-6
SCALE = HEAD_DIM ** -0.5
LOG2_E = float(np.log2(np.e))

CHUNK = 128
HEADS_PER_STEP = 4
HGRN_LEVELS = (64, 32, 16, 8, 4, 2, 1)
FFN_COL_TILES = 4
OUT_ROW_TILE = 512
OUT_COLS = 512
SB_SUB = 256
SB_ROW_GROUPS = 4
SB_DEAD_LOG2 = -160.0
PROJ_COL_TILES = 4
MXU_COLS = 256
VMEM_LIMIT = 56 * 1024 * 1024


def _cparams(sem):
    return pltpu.CompilerParams(dimension_semantics=sem, vmem_limit_bytes=VMEM_LIMIT)


def _dot(a, b):
    return jnp.dot(a, b, preferred_element_type=F32)


def _dot_nt(a, b):
    return lax.dot_general(a, b, (((1,), (1,)), ((), ())), preferred_element_type=F32)


def _dot_split(a, b01, terms):
    acc = None
    rem = a
    for _ in range(terms):
        piece = rem.astype(BF16)
        part = _dot(piece, b01)
        acc = part if acc is None else acc + part
        rem = rem - piece.astype(F32)
    return acc


def _dot_split_rhs(a01, b, terms):
    acc = None
    rem = b
    for _ in range(terms):
        piece = rem.astype(BF16)
        part = _dot(a01, piece)
        acc = part if acc is None else acc + part
        rem = rem - piece.astype(F32)
    return acc


def _dot_nt_split(a01, b, terms):
    acc = None
    rem = b
    for _ in range(terms):
        piece = rem.astype(BF16)
        part = _dot_nt(a01, piece)
        acc = part if acc is None else acc + part
        rem = rem - piece.astype(F32)
    return acc


def _softplus(z):
    return jnp.maximum(z, 0.0) + jnp.log(1.0 + jnp.exp(-jnp.abs(z)))


def _log_sigmoid(z):
    return -_softplus(-z)


def _sigmoid(z):
    return 1.0 / (1.0 + jnp.exp(-z))


def _iota(shape, dim):
    return lax.broadcasted_iota(jnp.int32, shape, dim)


def _rope_swap(x):
    lane = _iota(x.shape, x.ndim - 1)
    up = pltpu.roll(x, HEAD_DIM - ROPE_DIM // 2, axis=x.ndim - 1)
    down = pltpu.roll(x, ROPE_DIM // 2, axis=x.ndim - 1)
    return jnp.where(lane < ROPE_DIM // 2, up, down)


def _rope_tables(pos):
    half = ROPE_DIM // 2
    freqs = ROPE_THETA ** (-np.arange(half, dtype=np.float32) / half)
    ang = jnp.asarray(pos, F32)[:, None] * jnp.asarray(freqs, F32)[None, :]
    cos, sin = jnp.cos(ang), jnp.sin(ang)
    n = ang.shape[0]
    cos_t = jnp.concatenate([cos, cos, jnp.ones((n, HEAD_DIM - ROPE_DIM), F32)], axis=1)
    sin_t = jnp.concatenate([-sin, sin, jnp.zeros((n, HEAD_DIM - ROPE_DIM), F32)], axis=1)
    return cos_t, sin_t


def _rms_to_bf16(h_ref, g_ref):
    x = h_ref[...]
    ms = jnp.mean(x * x, axis=-1, keepdims=True)
    return (x * lax.rsqrt(ms + RMS_EPS) * g_ref[...]).astype(BF16)


def _norm_proj_kernel(h_ref, g_ref, w_ref, wg_ref, o_ref, og_ref, u_scr):
    @pl.when(pl.program_id(2) == 0)
    def _():
        u = _rms_to_bf16(h_ref, g_ref)
        u_scr[...] = u
        og_ref[...] = _dot(u, wg_ref[...])

    u = u_scr[...]
    tn = w_ref.shape[1]
    for c0 in range(0, tn, MXU_COLS):
        width = min(MXU_COLS, tn - c0)
        res = _dot(u, w_ref[:, c0:c0 + width])
        for c in range(width // HEAD_DIM):
            o_ref[c0 // HEAD_DIM + c] = res[:, c * HEAD_DIM:(c + 1) * HEAD_DIM].astype(o_ref.dtype)


def _norm_proj(h, g, w, wg, *, tm, tn):
    b, s, d = h.shape
    n = w.shape[1]
    ng = wg.shape[1]
    ncb = tn // HEAD_DIM
    return pl.pallas_call(
        _norm_proj_kernel,
        grid=(b, s // tm, n // tn),
        in_specs=[
            pl.BlockSpec((None, tm, d), lambda bi, si, j: (bi, si, 0)),
            pl.BlockSpec((1, d), lambda bi, si, j: (0, 0)),
            pl.BlockSpec((d, tn), lambda bi, si, j: (0, j)),
            pl.BlockSpec((d, ng), lambda bi, si, j: (0, 0)),
        ],
        out_specs=[
            pl.BlockSpec((None, ncb, tm, HEAD_DIM), lambda bi, si, j: (bi, j, si, 0)),
            pl.BlockSpec((None, tm, ng), lambda bi, si, j: (bi, si, 0)),
        ],
        out_shape=[
            jax.ShapeDtypeStruct((b, n // HEAD_DIM, s, HEAD_DIM), BF16),
            jax.ShapeDtypeStruct((b, s, ng), F32),
        ],
        scratch_shapes=[pltpu.VMEM((tm, d), BF16)],
        compiler_params=_cparams(("parallel", "parallel", "arbitrary")),
        name="norm_proj",
    )(h, g, w, wg)


def _ffn_up_kernel(u_ref, wg_ref, wu_ref, cw_ref, cb_ref, o_ref, halo_scr):
    si = pl.program_id(1)
    j = pl.program_id(2)
    tm, tn = o_ref.shape

    @pl.when(si == 0)
    def _():
        halo_scr[j] = jnp.zeros(halo_scr.shape[1:], F32)

    u = u_ref[...]
    cw = cw_ref[...]
    cb = cb_ref[...]
    for c0 in range(0, tn, HEAD_DIM):
        cols = slice(c0, c0 + HEAD_DIM)
        res = _dot(u, jnp.concatenate([wg_ref[:, cols], wu_ref[:, cols]], axis=1))
        gate = res[:, :HEAD_DIM]
        up = res[:, HEAD_DIM:]
        ext = jnp.concatenate([halo_scr[j, :, cols], gate], axis=0)
        halo_scr[j, :, cols] = gate[tm - 8:, :]
        conv = cb[:, cols] + cw[FFN_CONV - 1:FFN_CONV, cols] * gate
        for k in range(FFN_CONV - 1):
            off = 8 - (FFN_CONV - 1) + k
            conv = conv + cw[k:k + 1, cols] * ext[off:off + tm, :]
        o_ref[:, cols] = (conv * _sigmoid(conv) * up).astype(o_ref.dtype)


def _ffn_up(u, w_up, cw, cb, *, tm, tn):
    b, s, d = u.shape
    f = w_up.shape[1] // 2
    nj = f // tn
    return pl.pallas_call(
        _ffn_up_kernel,
        grid=(b, s // tm, nj),
        in_specs=[
            pl.BlockSpec((None, tm, d), lambda bi, si, j: (bi, si, 0)),
            pl.BlockSpec((d, tn), lambda bi, si, j: (0, j)),
            pl.BlockSpec((d, tn), lambda bi, si, j: (0, nj + j)),
            pl.BlockSpec((FFN_CONV, tn), lambda bi, si, j: (0, j)),
            pl.BlockSpec((1, tn), lambda bi, si, j: (0, j)),
        ],
        out_specs=pl.BlockSpec((None, tm, tn), lambda bi, si, j: (bi, si, j)),
        out_shape=jax.ShapeDtypeStruct((b, s, f), BF16),
        scratch_shapes=[pltpu.VMEM((f // tn, 8, tn), F32)],
        compiler_params=_cparams(("parallel", "arbitrary", "arbitrary")),
        name="ffn_up",
    )(u, w_up, w_up, cw, cb)


def _out_proj_norm_kernel(h_ref, a1_ref, w1_ref, a2_ref, w2_ref, g_ref, ho_ref, u_ref):
    tm, d = ho_ref.shape
    a1 = a1_ref[...]
    a2 = a2_ref[...]
    ss = jnp.zeros((tm, 1), F32)
    for c0 in range(0, d, OUT_COLS):
        cols = slice(c0, c0 + OUT_COLS)
        acc = h_ref[:, cols] + _dot(a1, w1_ref[:, cols]) + _dot(a2, w2_ref[:, cols])
        ho_ref[:, cols] = acc
        ss = ss + jnp.sum(acc * acc, axis=1, keepdims=True)
    inv = lax.rsqrt(ss * (1.0 / d) + RMS_EPS)
    for c0 in range(0, d, OUT_COLS):
        cols = slice(c0, c0 + OUT_COLS)
        u_ref[:, cols] = (ho_ref[:, cols] * inv * g_ref[:, cols]).astype(u_ref.dtype)


def _out_proj_norm(h, a1, a2, w, g, *, tm):
    b, s, d = h.shape
    k = a1.shape[-1]
    assert a2.shape[-1] == k and w.shape[0] == 2 * k
    rows = lambda n: pl.BlockSpec((None, tm, n), lambda bi, si: (bi, si, 0))
    half = lambda i: pl.BlockSpec((k, d), lambda bi, si: (i, 0))
    return pl.pallas_call(
        _out_proj_norm_kernel,
        grid=(b, s // tm),
        in_specs=[rows(d), rows(k), half(0), rows(k), half(1), pl.BlockSpec((1, d), lambda bi, si: (0, 0))],
        out_specs=[rows(d), rows(d)],
        out_shape=[jax.ShapeDtypeStruct((b, s, d), F32), jax.ShapeDtypeStruct((b, s, d), BF16)],
        compiler_params=_cparams(("parallel", "parallel")),
        name="out_proj_norm",
    )(h, a1, w, a2, w, g)


def _proj_res_kernel(*refs):
    h_ref, o_ref = refs[0], refs[-1]
    ops = refs[1:-1]
    acc = h_ref[...]
    for i in range(0, len(ops), 2):
        acc = acc + _dot(ops[i][...], ops[i + 1][...])
    o_ref[...] = acc


def _proj_res(h, pairs, *, tm, tn):
    b, s, d = h.shape
    in_specs = [pl.BlockSpec((None, tm, tn), lambda bi, si, j: (bi, si, j))]
    args = [h]
    for a, w in pairs:
        k = a.shape[-1]
        in_specs.append(pl.BlockSpec((None, tm, k), lambda bi, si, j: (bi, si, 0)))
        in_specs.append(pl.BlockSpec((k, tn), lambda bi, si, j: (0, j)))
        args += [a, w]
    return pl.pallas_call(
        _proj_res_kernel,
        grid=(b, s // tm, d // tn),
        in_specs=in_specs,
        out_specs=pl.BlockSpec((None, tm, tn), lambda bi, si, j: (bi, si, j)),
        out_shape=jax.ShapeDtypeStruct((b, s, d), F32),
        compiler_params=_cparams(("parallel", "parallel", "arbitrary")),
        name="proj_res",
    )(*args)


def _rmsnorm_kernel(h_ref, g_ref, o_ref):
    x = h_ref[...]
    ms = jnp.mean(x * x, axis=-1, keepdims=True)
    o_ref[...] = x * lax.rsqrt(ms + RMS_EPS) * g_ref[...]


def _rmsnorm(h, g, *, tm):
    b, s, d = h.shape
    return pl.pallas_call(
        _rmsnorm_kernel,
        grid=(b, s // tm),
        in_specs=[pl.BlockSpec((None, tm, d), lambda bi, si: (bi, si, 0)),
                  pl.BlockSpec((1, d), lambda bi, si: (0, 0))],
        out_specs=pl.BlockSpec((None, tm, d), lambda bi, si: (bi, si, 0)),
        out_shape=jax.ShapeDtypeStruct((b, s, d), F32),
        compiler_params=_cparams(("parallel", "parallel")),
        name="final_norm",
    )(h, g)


def _mlstm_kernel(q_ref, k_ref, v_ref, og_ref, ig_ref, fg_ref, bi_ref, bf_ref,
                  cwq_ref, cwk_ref, cbq_ref, cbk_ref, hn_ref, out_ref,
                  pad_scr, qc_scr, kc_scr, c_scr, b_scr, d_scr):
    nh, s, _ = q_ref.shape
    nch = s // CHUNK

    def conv_silu(src_ref, cw_ref, cb_ref, dst_ref, hh, scale):
        lanes = slice(hh * HEAD_DIM, (hh + 1) * HEAD_DIM)
        pad_scr[0:8, :] = jnp.zeros((8, HEAD_DIM), F32)
        for p in range(nch):
            pad_scr[8 + p * CHUNK:8 + (p + 1) * CHUNK, :] = src_ref[hh, p * CHUNK:(p + 1) * CHUNK, :].astype(F32)
        cw = cw_ref[:, lanes]
        for p in range(nch):
            acc = cb_ref[:, lanes] + cw[0:1, :] * pad_scr[pl.ds(8 - (A_CONV - 1) + p * CHUNK, CHUNK), :]
            for j in range(1, A_CONV):
                acc = acc + cw[j:j + 1, :] * pad_scr[pl.ds(8 - (A_CONV - 1) + j + p * CHUNK, CHUNK), :]
            y = acc * _sigmoid(acc)
            if scale != 1.0:
                y = y * scale
            dst_ref[hh, p * CHUNK:(p + 1) * CHUNK, :] = y.astype(BF16)

    r_i = _iota((CHUNK, CHUNK), 0)
    c_i = _iota((CHUNK, CHUNK), 1)
    upper = (r_i <= c_i).astype(BF16)
    for hh in range(nh):
        conv_silu(q_ref, cwq_ref, cbq_ref, qc_scr, hh, 1.0)
        conv_silu(k_ref, cwk_ref, cbk_ref, kc_scr, hh, SCALE)
        log_f = _log_sigmoid(fg_ref[hh] + bf_ref[hh])
        log_i = ig_ref[hh] + bi_ref[hh]
        b_all = _dot_split(log_f, upper, 3)
        b_scr[hh] = b_all
        d_scr[hh] = log_i - b_all
    c_scr[...] = jnp.zeros(c_scr.shape, F32)
    causal = c_i <= r_i
    ones_blk = jnp.ones((CHUNK, HEAD_DIM), BF16)

    def head_chunk(hh, c, t0, m):
        lanes = slice(hh * HEAD_DIM, (hh + 1) * HEAD_DIM)
        q = qc_scr[hh, pl.ds(t0, CHUNK), :]
        k = kc_scr[hh, pl.ds(t0, CHUNK), :]
        v_aug = jnp.concatenate([v_ref[hh, pl.ds(t0, CHUNK), :], ones_blk], axis=1)
        b_row = b_scr[hh, pl.ds(c, 1), :]
        d_row = d_scr[hh, pl.ds(c, 1), :]
        b_col = jnp.transpose(jnp.broadcast_to(b_row, (CHUNK, CHUNK)))
        d_col = jnp.transpose(jnp.broadcast_to(d_row, (CHUNK, CHUNK)))
        dmat = jnp.where(causal, b_col + d_row, -jnp.inf)
        g = b_col + m
        m_row = jnp.maximum(g, jnp.max(dmat, axis=1, keepdims=True))
        w = jnp.exp(dmat - m_row) * _dot_nt(q, k)
        w_state = jnp.exp(g - m_row)
        c_old = c_scr[hh]
        inter = _dot(q, c_old.astype(BF16))
        intra = _dot(w.astype(BF16), v_aug)
        tot = jnp.concatenate([w_state, w_state], axis=1) * inter + intra
        num = tot[:, :HEAD_DIM]
        den = tot[:, HEAD_DIM:]
        hv = num / jnp.maximum(jnp.abs(den), jnp.exp(-m_row))
        y = hv * lax.rsqrt(jnp.mean(hv * hv, axis=-1, keepdims=True) + RMS_EPS) * hn_ref[:, lanes]
        y = y * _sigmoid(og_ref[hh, pl.ds(t0, CHUNK), :].astype(F32))
        out_ref[pl.ds(t0, CHUNK), lanes] = y.astype(out_ref.dtype)
        b_last = b_col[CHUNK - 1:CHUNK, :]
        w_new = b_last + d_col
        m_new = jnp.maximum(b_last + m, jnp.max(w_new, axis=0, keepdims=True))
        decay = jnp.exp(b_last + m - m_new)
        kw = jnp.exp(w_new - m_new) * k.astype(F32)
        kw_t = jnp.transpose(kw).astype(BF16)
        c_scr[hh] = jnp.concatenate([decay, decay], axis=1) * c_old + _dot(kw_t, v_aug)
        return m_new

    def body(c, ms):
        t0 = pl.multiple_of(c * CHUNK, CHUNK)
        return tuple(head_chunk(hh, c, t0, ms[hh]) for hh in range(nh))

    lax.fori_loop(0, nch, body, tuple(jnp.zeros((1, CHUNK), F32) for _ in range(nh)))


def _mlstm(p, gi, gf, bias_i, bias_f, cw, cb, hn):
    b, _, s, _ = p.shape
    nch = s // CHUNK
    nh = HEADS_PER_STEP
    nblk = N_HEADS // nh
    w = nh * HEAD_DIM
    slab = lambda off: pl.BlockSpec((None, nh, s, HEAD_DIM), lambda bi, hi: (bi, off + hi, 0, 0))
    gate = pl.BlockSpec((None, nh, nch, CHUNK), lambda bi, hi: (bi, hi, 0, 0))
    hb = pl.BlockSpec((nh, 1, CHUNK), lambda bi, hi: (hi, 0, 0))
    return pl.pallas_call(
        _mlstm_kernel,
        grid=(b, nblk),
        in_specs=[
            slab(0), slab(nblk), slab(2 * nblk), slab(3 * nblk), gate, gate, hb, hb,
            pl.BlockSpec((A_CONV, w), lambda bi, hi: (0, hi)),
            pl.BlockSpec((A_CONV, w), lambda bi, hi: (0, nblk + hi)),
            pl.BlockSpec((1, w), lambda bi, hi: (0, hi)),
            pl.BlockSpec((1, w), lambda bi, hi: (0, nblk + hi)),
            pl.BlockSpec((1, w), lambda bi, hi: (0, hi)),
        ],
        out_specs=pl.BlockSpec((None, s, w), lambda bi, hi: (bi, 0, hi)),
        out_shape=jax.ShapeDtypeStruct((b, s, N_HEADS * HEAD_DIM), BF16),
        scratch_shapes=[
            pltpu.VMEM((s + 8, HEAD_DIM), F32),
            pltpu.VMEM((nh, s, HEAD_DIM), BF16),
            pltpu.VMEM((nh, s, HEAD_DIM), BF16),
            pltpu.VMEM((nh, HEAD_DIM, 2 * HEAD_DIM), F32),
            pltpu.VMEM((nh, nch, CHUNK), F32),
            pltpu.VMEM((nh, nch, CHUNK), F32),
        ],
        compiler_params=_cparams(("parallel", "parallel")),
        name="mlstm",
    )(p, p, p, p, gi, gf, bias_i, bias_f, cw, cw, cb, cb, hn)


def _sb_kernel(q_ref, k_ref, v_ref, o_ref):
    qi = pl.program_id(2)
    n_rg = SB_ROW_GROUPS
    sub = q_ref.shape[0] // n_rg
    r_i = _iota((sub, sub), 0)
    c_i = _iota((sub, sub), 1)
    suffix = (r_i > c_i).astype(BF16)
    strict = c_i < r_i
    qs = [q_ref[g * sub:(g + 1) * sub, :] for g in range(n_rg)]
    j_diag = [n_rg * qi + g for g in range(n_rg)]

    def sub_block(q, j, run, diag):
        k0 = pl.multiple_of(j * sub, sub)
        k = k_ref[pl.ds(k0, sub), :]
        v = v_ref[pl.ds(k0, sub), :]
        z2 = _dot_nt(q, k)
        sp2 = jnp.maximum(z2, 0.0) + jnp.log2(1.0 + jnp.exp2(-jnp.abs(z2)))
        log_keep = -sp2
        if diag:
            log_keep = jnp.where(strict, log_keep, 0.0)
        later = _dot(log_keep.astype(BF16), suffix) + run
        a = jnp.exp2(z2 - sp2 + later)
        if diag:
            a = jnp.where(strict, a, 0.0)
        return _dot(a.astype(BF16), v), jnp.sum(log_keep, axis=1, keepdims=True)

    def any_alive(runs, n_next):
        top = jnp.full((sub, 1), -jnp.inf, F32)
        for g in range(n_rg):
            top = jnp.maximum(top, jnp.where(j_diag[g] >= n_next, runs[g], -jnp.inf))
        return (jnp.max(top) > SB_DEAD_LOG2).astype(jnp.int32)

    accs, runs = [], []
    for g in range(n_rg):
        o, sm = sub_block(qs[g], j_diag[g], 0.0, True)
        accs.append(o)
        runs.append(sm)

    def cond(carry):
        return carry[1] > 0

    def body(carry):
        n = carry[0]
        accs, runs = list(carry[2]), list(carry[3])
        for g in range(n_rg):
            j = j_diag[g] - n
            o, sm = sub_block(qs[g], jnp.maximum(j, 0), runs[g], False)
            live = j >= 0
            accs[g] = accs[g] + jnp.where(live, o, 0.0)
            runs[g] = runs[g] + jnp.where(live, sm, 0.0)
        return n + 1, any_alive(runs, n + 1), tuple(accs), tuple(runs)

    out = lax.while_loop(cond, body, (jnp.int32(1), any_alive(runs, 1), tuple(accs), tuple(runs)))
    for g in range(n_rg):
        o_ref[g * sub:(g + 1) * sub, :] = out[2][g].astype(o_ref.dtype)


def _stick_breaking(p, *, q_off, k_off, v_off, tq):
    b, _, s, _ = p.shape
    return pl.pallas_call(
        _sb_kernel,
        grid=(b, N_HEADS, s // tq),
        in_specs=[
            pl.BlockSpec((None, None, tq, HEAD_DIM), lambda bi, hi, qi: (bi, q_off + hi, qi, 0)),
            pl.BlockSpec((None, None, s, HEAD_DIM), lambda bi, hi, qi: (bi, k_off + hi, 0, 0)),
            pl.BlockSpec((None, None, s, HEAD_DIM), lambda bi, hi, qi: (bi, v_off + hi, 0, 0)),
        ],
        out_specs=pl.BlockSpec((None, tq, HEAD_DIM), lambda bi, hi, qi: (bi, qi, hi)),
        out_shape=jax.ShapeDtypeStruct((b, s, N_HEADS * HEAD_DIM), BF16),
        compiler_params=_cparams(("parallel", "parallel", "arbitrary")),
        name="stick_breaking",
    )(p, p, p)


def _gelu_tanh(x):
    return 0.5 * x * (1.0 + jnp.tanh(np.sqrt(2.0 / np.pi).astype(np.float32) * (x + 0.044715 * (x * x * x))))


def _nsa_prep_kernel(kcs_ref, vcs_ref, ks_ref, kw_ref, pos_ref, w1_ref, w2_ref,
                     cos_ref, sin_ref, cosc_ref, sinc_ref,
                     kc_out, vc_out, ks_out, kw_out, f32_scr):
    s = ks_ref.shape[0]
    ng = s // CMP_STRIDE

    def compress(src_ref, idx):
        f32_scr[...] = src_ref[...].astype(F32)
        xa = jnp.zeros((ng, HEAD_DIM), F32)
        xb = jnp.zeros((ng, HEAD_DIM), F32)
        for l in range(CMP_STRIDE):
            rows = f32_scr[pl.ds(l, ng, stride=CMP_STRIDE), :]
            xa = xa + _dot((rows + pos_ref[idx, l:l + 1, :]).astype(BF16), w1_ref[idx, l])
            xb = xb + _dot((rows + pos_ref[idx, CMP_STRIDE + l:CMP_STRIDE + l + 1, :]).astype(BF16),
                           w1_ref[idx, CMP_STRIDE + l])
        hid = _gelu_tanh(xa + pltpu.roll(xb, ng - 1, axis=0))
        return _dot(hid.astype(BF16), w2_ref[idx])

    kc = compress(kcs_ref, 0)
    kc_out[...] = (kc * cosc_ref[...] + _rope_swap(kc) * sinc_ref[...]).astype(kc_out.dtype)
    vc_out[...] = compress(vcs_ref, 1).astype(vc_out.dtype)
    for src, dst in ((ks_ref, ks_out), (kw_ref, kw_out)):
        x = src[...].astype(F32)
        dst[...] = (x * cos_ref[...] + _rope_swap(x) * sin_ref[...]).astype(dst.dtype)


def _nsa_prep(p, pos, w1, w2, cos, sin, cosc, sinc, *, kc_off, vc_off, ks_off, kw_off):
    b, _, s, _ = p.shape
    ng = s // CMP_STRIDE
    slab = lambda off: pl.BlockSpec((None, None, s, HEAD_DIM), lambda bi, gi: (bi, off + gi, 0, 0))
    full = lambda a: pl.BlockSpec(a.shape, lambda bi, gi: (0,) * a.ndim)
    out_c = pl.BlockSpec((None, None, ng, HEAD_DIM), lambda bi, gi: (bi, gi, 0, 0))
    out_s = pl.BlockSpec((None, None, s, HEAD_DIM), lambda bi, gi: (bi, gi, 0, 0))
    return pl.pallas_call(
        _nsa_prep_kernel,
        grid=(b, KV_GROUPS),
        in_specs=[slab(kc_off), slab(vc_off), slab(ks_off), slab(kw_off),
                  full(pos), full(w1), full(w2), full(cos), full(sin), full(cosc), full(sinc)],
        out_specs=[out_c, out_c, out_s, out_s],
        out_shape=[jax.ShapeDtypeStruct((b, KV_GROUPS, ng, HEAD_DIM), BF16),
                   jax.ShapeDtypeStruct((b, KV_GROUPS, ng, HEAD_DIM), BF16),
                   jax.ShapeDtypeStruct((b, KV_GROUPS, s, HEAD_DIM), BF16),
                   jax.ShapeDtypeStruct((b, KV_GROUPS, s, HEAD_DIM), BF16)],
        scratch_shapes=[pltpu.VMEM((s, HEAD_DIM), F32)],
        compiler_params=_cparams(("parallel", "parallel")),
        name="nsa_prep",
    )(p, p, p, p, pos, w1, w2, cos, sin, cosc, sinc)


def _masked_softmax(sc, mask):
    sc = jnp.where(mask, sc, -jnp.inf)
    m = jnp.max(sc, axis=-1, keepdims=True)
    m = jnp.where(m > -jnp.inf, m, 0.0)
    e = jnp.exp(sc - m)
    return e / jnp.maximum(jnp.sum(e, axis=-1, keepdims=True), 1e-30)


def _nsa_kernel(q_ref, kc_ref, vc_ref, ks_ref, vs_ref, kw_ref, vw_ref, gate_ref, cos_ref, sin_ref,
                cover_t_ref, o_ref, *, tk):
    qi = pl.program_id(2)
    r = q_ref.shape[0]
    tq = q_ref.shape[1]
    s_len = ks_ref.shape[0]
    n_cmp = (s_len - CMP_LEN) // CMP_STRIDE + 1
    t0 = qi * tq

    qf = q_ref[...].astype(F32).reshape(r * tq, HEAD_DIM)
    cos = jnp.concatenate([cos_ref[...]] * r, axis=0)
    sin = jnp.concatenate([sin_ref[...]] * r, axis=0)
    qb = (qf * cos + _rope_swap(qf) * sin).astype(BF16)

    t_col = t0 + _iota((tq, HEAD_DIM), 0)
    lane = _iota((tq, HEAD_DIM), 1)

    ncl = kc_ref.shape[0]
    sc = _dot_nt(qb, kc_ref[...]).reshape(r, tq, ncl)
    cmp_ok = ((lane * CMP_STRIDE + (CMP_LEN - 1) <= t_col) & (lane < n_cmp))[:, :ncl]
    p_cmp = _masked_softmax(sc, cmp_ok[None])
    o_cmp = _dot(p_cmp.reshape(r * tq, ncl).astype(BF16), vc_ref[...]).reshape(r, tq, HEAD_DIM)

    band = WINDOW + tq
    w0 = pl.multiple_of(jnp.maximum(t0 - WINDOW, 0), tq)
    k_win = kw_ref[pl.ds(w0, band), :]
    v_win = vw_ref[pl.ds(w0, band), :]
    diff = (t0 + _iota((tq, band), 0)) - (w0 + _iota((tq, band), 1))
    win_ok = (diff >= 0) & (diff < WINDOW)
    sw = _dot_nt(qb, k_win).reshape(r, tq, band)
    p_win = _masked_softmax(sw, win_ok[None])
    o_win = _dot(p_win.reshape(r * tq, band).astype(BF16), v_win).reshape(r, tq, HEAD_DIM)

    p_sum = p_cmp[0]
    for h in range(1, r):
        p_sum = p_sum + p_cmp[h]
    n_blk = s_len // SEL_LEN
    p_slc_t = _dot_nt_split(cover_t_ref[...], p_sum, 3)[:n_blk, :]
    blk = _iota((n_blk, tq), 0)
    blk_t = (t0 + _iota((n_blk, tq), 1)) >> SEL_SHIFT
    started = blk <= blk_t
    forced = (blk == 0) | (started & (blk_t - blk < N_LOCAL))
    score = jnp.where(started, p_slc_t + jnp.where(forced, FORCE_BONUS, 0.0), -jnp.inf)
    rank = jnp.zeros((n_blk, tq), F32)
    for jp in range(n_blk):
        row = score[jp:jp + 1, :]
        beats = (row > score) | ((row == score) & (blk > jp))
        rank = rank + jnp.where(beats, 1.0, 0.0)
    sel_t = jnp.where((rank < N_SEL) & started, 1.0, 0.0)
    sel_t = jnp.concatenate([sel_t, jnp.zeros((HEAD_DIM - n_blk, tq), F32)], axis=0)
    sel = jnp.transpose(sel_t).astype(BF16)

    e_row = _iota((HEAD_DIM, tk), 0)
    e_blk = _iota((HEAD_DIM, tk), 1) >> SEL_SHIFT
    s_loc = _iota((tq, tk), 1)
    t_q = t0 + _iota((tq, tk), 0)
    n_chain = r
    hc = r // n_chain
    q_chain = [qb[c * hc * tq:(c + 1) * hc * tq, :] for c in range(n_chain)]
    ones_blk = jnp.ones((tk, HEAD_DIM), BF16)

    def sel_tile(kt, carry):
        k0 = pl.multiple_of(kt * tk, tk)
        k = ks_ref[pl.ds(k0, tk), :]
        v_aug = jnp.concatenate([vs_ref[pl.ds(k0, tk), :], ones_blk], axis=1)
        expand = (e_row == e_blk + kt * (tk // SEL_LEN)).astype(BF16)
        ok = (_dot(sel, expand) > 0.5) & (k0 + s_loc <= t_q)
        bias = jnp.where(ok, 0.0, -jnp.inf)[None]
        new = []
        for c in range(n_chain):
            m, acc = carry[c]
            sc_t = _dot_nt(q_chain[c], k).reshape(hc, tq, tk) + bias
            m_new = jnp.maximum(m, jnp.max(sc_t, axis=-1, keepdims=True))
            alpha = jnp.exp(m - m_new)
            e = jnp.exp(sc_t - m_new)
            pv = _dot(e.reshape(hc * tq, tk).astype(BF16), v_aug).reshape(hc, tq, 2 * HEAD_DIM)
            new.append((m_new, alpha * acc + pv))
        return tuple(new)

    n_tiles = (t0 + tq + tk - 1) // tk
    init = tuple((jnp.full((hc, tq, 1), -jnp.inf, F32), jnp.zeros((hc, tq, 2 * HEAD_DIM), F32))
                 for _ in range(n_chain))
    chains = lax.fori_loop(0, n_tiles, sel_tile, init)

    gates = _sigmoid(gate_ref[...])
    for h in range(r):
        acc_s = chains[h // hc][1][h % hc]
        o_slc = acc_s[:, :HEAD_DIM] / jnp.maximum(acc_s[:, HEAD_DIM:], 1e-30)
        out = (gates[:, 3 * h:3 * h + 1] * o_cmp[h] + gates[:, 3 * h + 1:3 * h + 2] * o_slc
               + gates[:, 3 * h + 2:3 * h + 3] * o_win[h])
        o_ref[:, h * HEAD_DIM:(h + 1) * HEAD_DIM] = out.astype(o_ref.dtype)


def _nsa(p, kc, vc, ks, kw, gates, cos, sin, cover, *, vs_off, vw_off, tq, tk):
    b, _, s, _ = p.shape
    r = GROUP_HEADS
    ng = kc.shape[2]
    kv = lambda: pl.BlockSpec((None, None, s, HEAD_DIM), lambda bi, gi, qi: (bi, gi, 0, 0))
    pslab = lambda off: pl.BlockSpec((None, None, s, HEAD_DIM), lambda bi, gi, qi: (bi, off + gi, 0, 0))
    cmp_spec = pl.BlockSpec((None, None, ng, HEAD_DIM), lambda bi, gi, qi: (bi, gi, 0, 0))
    return pl.pallas_call(
        functools.partial(_nsa_kernel, tk=tk),
        grid=(b, KV_GROUPS, s // tq),
        in_specs=[
            pl.BlockSpec((None, r, tq, HEAD_DIM), lambda bi, gi, qi: (bi, gi, qi, 0)),
            cmp_spec, cmp_spec, kv(), pslab(vs_off), kv(), pslab(vw_off),
            pl.BlockSpec((None, tq, HEAD_DIM), lambda bi, gi, qi: (bi, qi, gi)),
            pl.BlockSpec((tq, HEAD_DIM), lambda bi, gi, qi: (qi, 0)),
            pl.BlockSpec((tq, HEAD_DIM), lambda bi, gi, qi: (qi, 0)),
            pl.BlockSpec(cover.shape, lambda bi, gi, qi: (0, 0)),
        ],
        out_specs=pl.BlockSpec((None, tq, r * HEAD_DIM), lambda bi, gi, qi: (bi, qi, gi)),
        out_shape=jax.ShapeDtypeStruct((b, s, N_HEADS * HEAD_DIM), BF16),
        compiler_params=_cparams(("parallel", "parallel", "arbitrary")),
        name="nsa",
    )(p, kc, vc, ks, p, kw, p, gates, cos, sin, cover)


def _hgrn_tables():
    r = np.arange(CHUNK)[:, None]
    j = np.arange(CHUNK)[None, :]
    mats = []
    for sz in HGRN_LEVELS:
        mid = (r // (2 * sz)) * (2 * sz) + sz
        right = (r // sz) % 2 == 1
        mats.append(np.where(right, (j >= mid) & (j <= r), (j > r) & (j <= mid - 1)))
    mats.append(j <= r)
    mats.append(j > r)
    return np.concatenate(mats, axis=0).astype(np.float32)


def _hgrn_kernel(q_ref, f_ref, i_ref, g_ref, lb_ref, hn_ref, tab_ref, out_ref, st_scr):
    nh, s, _ = q_ref.shape
    nch = s // CHUNK
    nlev = len(HGRN_LEVELS)
    r_i = _iota((CHUNK, CHUNK), 0)
    c_i = _iota((CHUNK, CHUNK), 1)
    row = _iota((CHUNK, HEAD_DIM), 0)
    st_scr[...] = jnp.zeros(st_scr.shape, F32)

    def head_chunk(hh, t0):
        lanes = slice(hh * HEAD_DIM, (hh + 1) * HEAD_DIM)
        lb = lb_ref[:, lanes]
        q = q_ref[hh, pl.ds(t0, CHUNK), :].astype(F32)
        fp = f_ref[hh, pl.ds(t0, CHUNK), :].astype(F32)
        v = i_ref[hh, pl.ds(t0, CHUNK), :]
        log_lb = jnp.log(lb)
        a2 = jnp.log1p(-lb) - _softplus(-fp)
        hi = jnp.maximum(log_lb, a2)
        lo = jnp.minimum(log_lb, a2)
        log_f = hi + jnp.log(1.0 + jnp.exp(lo - hi))
        k = (1.0 - lb) * _sigmoid(-fp)
        x_all = _dot_split_rhs(tab_ref[...], log_f, 2)
        att = jnp.where(r_i == c_i, jnp.sum(q * k, axis=1, keepdims=True), 0.0)
        for lv, sz in enumerate(HGRN_LEVELS):
            sh = sz.bit_length() - 1
            e = jnp.exp(x_all[lv * CHUNK:(lv + 1) * CHUNK, :])
            right = ((row >> sh) & 1) == 1
            qt = jnp.where(right, q * e, 0.0).astype(BF16)
            kt = jnp.where(right, 0.0, k * e).astype(BF16)
            same = (r_i >> (sh + 1)) == (c_i >> (sh + 1))
            att = att + jnp.where(same, _dot_nt(qt, kt), 0.0)
        bc = x_all[nlev * CHUNK:(nlev + 1) * CHUNK, :]
        rev = x_all[(nlev + 1) * CHUNK:(nlev + 2) * CHUNK, :]
        st = st_scr[hh]
        o = _dot(att.astype(BF16), v) + _dot_nt((q * jnp.exp(bc)).astype(BF16), st.astype(BF16))
        y = o * lax.rsqrt(jnp.mean(o * o, axis=-1, keepdims=True) + RMS_EPS) * hn_ref[:, lanes]
        gg = g_ref[hh, pl.ds(t0, CHUNK), :].astype(F32)
        out_ref[pl.ds(t0, CHUNK), lanes] = (y * (gg * _sigmoid(gg))).astype(out_ref.dtype)
        v_t = jnp.transpose(v.astype(F32)).astype(BF16)
        st_scr[hh] = st * jnp.exp(bc[CHUNK - 1:CHUNK, :]) + _dot(v_t, (k * jnp.exp(rev)).astype(BF16))

    def body(c, carry):
        t0 = pl.multiple_of(c * CHUNK, CHUNK)
        for hh in range(nh):
            head_chunk(hh, t0)
        return carry

    lax.fori_loop(0, nch, body, 0)


def _hgrn(p, lb, hn, tab, *, q_off, f_off, i_off, g_off):
    b, _, s, _ = p.shape
    nh = HEADS_PER_STEP
    nblk = N_HEADS // nh
    w = nh * HEAD_DIM
    slab = lambda off: pl.BlockSpec((None, nh, s, HEAD_DIM), lambda bi, hi: (bi, off // nh + hi, 0, 0))
    return pl.pallas_call(
        _hgrn_kernel,
        grid=(b, nblk),
        in_specs=[slab(q_off), slab(f_off), slab(i_off), slab(g_off),
                  pl.BlockSpec((1, w), lambda bi, hi: (0, hi)),
                  pl.BlockSpec((1, w), lambda bi, hi: (0, hi)),
                  pl.BlockSpec(tab.shape, lambda bi, hi: (0, 0))],
        out_specs=pl.BlockSpec((None, s, w), lambda bi, hi: (bi, 0, hi)),
        out_shape=jax.ShapeDtypeStruct((b, s, N_HEADS * HEAD_DIM), BF16),
        scratch_shapes=[pltpu.VMEM((nh, HEAD_DIM, HEAD_DIM), F32)],
        compiler_params=_cparams(("parallel", "parallel")),
        name="hgrn2",
    )(p, p, p, p, lb, hn, tab)


def _pad_cols(w, n):
    return jnp.pad(w, ((0, 0), (0, n - w.shape[1])))


def _row_tile(s):
    return min(1024, s)


def _layer_ab(h, norm_g, w_in, conv_w, conv_b, gate_b, head_norm, w_out, ffn_norm_g):
    b, s, d = h.shape
    aw = N_HEADS * HEAD_DIM
    tm = _row_tile(s)
    g0 = 4 * aw
    g1 = g0 + 2 * N_HEADS
    w_main = jnp.concatenate([w_in[:, :g0], w_in[:, g1:g1 + aw] * (SCALE * LOG2_E), w_in[:, g1 + aw:]],
                             axis=1).astype(BF16)
    w_gate = _pad_cols(w_in[:, g0:g1], HEAD_DIM).astype(BF16)
    p, gates = _norm_proj(h, norm_g.reshape(1, d), w_main, w_gate, tm=tm, tn=w_main.shape[1] // PROJ_COL_TILES)
    nch = s // CHUNK
    g_t = jnp.transpose(gates[:, :, :2 * N_HEADS], (0, 2, 1))
    gi = g_t[:, :N_HEADS].reshape(b, N_HEADS, nch, CHUNK)
    gf = g_t[:, N_HEADS:].reshape(b, N_HEADS, nch, CHUNK)
    bias_i = jnp.broadcast_to(gate_b[:N_HEADS, None, None], (N_HEADS, 1, CHUNK)).astype(F32)
    bias_f = jnp.broadcast_to(gate_b[N_HEADS:, None, None], (N_HEADS, 1, CHUNK)).astype(F32)
    h_a = _mlstm(p, gi, gf, bias_i, bias_f, conv_w, conv_b.reshape(1, -1), head_norm.reshape(1, -1))
    h_b = _stick_breaking(p, q_off=4 * N_HEADS, k_off=5 * N_HEADS, v_off=6 * N_HEADS, tq=min(SB_ROW_GROUPS * SB_SUB, s))
    w_o = w_out.astype(BF16)
    return _out_proj_norm(h, h_a, h_b, w_o, ffn_norm_g.reshape(1, d), tm=min(OUT_ROW_TILE, s))


def _layer_cd(h, norm_g, w_in, cmp_pos, cmp_w1, cmp_w2, lower_bound, head_norm, w_out, ffn_norm_g):
    b, s, d = h.shape
    cw = N_HEADS * HEAD_DIM
    kvw = KV_GROUPS * HEAD_DIM
    tm = _row_tile(s)
    n_gate = 3 * N_HEADS
    g0 = cw + 6 * kvw
    w_main = jnp.concatenate([w_in[:, :cw] * SCALE, w_in[:, cw:g0], w_in[:, g0 + n_gate:]], axis=1).astype(BF16)
    per_group = 3 * GROUP_HEADS
    w_gate = jnp.concatenate(
        [_pad_cols(w_in[:, g0 + g * per_group:g0 + (g + 1) * per_group], HEAD_DIM) for g in range(KV_GROUPS)],
        axis=1).astype(BF16)
    p, gates = _norm_proj(h, norm_g.reshape(1, d), w_main, w_gate, tm=tm, tn=w_main.shape[1] // PROJ_COL_TILES)
    pos = cmp_pos.astype(F32)
    w1 = cmp_w1.astype(BF16)
    w2 = cmp_w2.astype(BF16)
    cos, sin = _rope_tables(np.arange(s))
    ng = s // CMP_STRIDE
    cosc, sinc = _rope_tables(np.arange(ng) * CMP_STRIDE + CMP_LEN - 1)
    kc, vc, ks, kw = _nsa_prep(p, pos, w1, w2, cos, sin, cosc, sinc,
                               kc_off=8, vc_off=10, ks_off=12, kw_off=16)
    n_cmp = (s - CMP_LEN) // CMP_STRIDE + 1
    n_sel = s // SEL_LEN
    cs = np.arange(n_cmp) * CMP_STRIDE
    ss = np.arange(n_sel) * SEL_LEN
    cover_t = np.zeros((HEAD_DIM, ng), np.float32)
    cover_t[:n_sel, :n_cmp] = (cs[None, :] < ss[:, None] + SEL_LEN) & (cs[None, :] + CMP_LEN > ss[:, None])
    o_c = _nsa(p, kc, vc, ks, kw, gates, cos, sin, jnp.asarray(cover_t, BF16),
               vs_off=14, vw_off=18, tq=256, tk=min(512, s))
    o_d = _hgrn(p, lower_bound.reshape(1, -1).astype(F32), head_norm.reshape(1, -1),
                jnp.asarray(_hgrn_tables(), BF16), q_off=20, f_off=28, i_off=36, g_off=44)
    w_o = w_out.astype(BF16)
    return _out_proj_norm(h, o_c, o_d, w_o, ffn_norm_g.reshape(1, d), tm=min(OUT_ROW_TILE, s))


def _ffn(h, u, w_up, conv_w, conv_b, w_down):
    b, s, d = h.shape
    f = w_down.shape[0]
    tm = _row_tile(s)
    act = _ffn_up(u, w_up.astype(BF16), conv_w, conv_b.reshape(1, f), tm=tm, tn=f // FFN_COL_TILES)
    return _proj_res(h, [(act, w_down.astype(BF16))], tm=tm, tn=512)


def kernel(x, norm_mix, norm_ffn, norm_final, ab_w_in, ab_conv_w, ab_conv_b, ab_gate_b, ab_head_norm, ab_w_out, cd_w_in, cd_cmp_pos, cd_cmp_w1, cd_cmp_w2, hgrn_gamma, cd_head_norm, cd_w_out, ffn_w_up, ffn_conv_w, ffn_conv_b, ffn_w_down):
    depth = norm_mix.shape[0]
    sm = jax.nn.softmax(hgrn_gamma.astype(F32), axis=0)
    lb_all = jnp.cumsum(sm, axis=0) - sm[0]
    h = x
    for layer in range(depth):
        j = layer // 2
        if layer % 2 == 0:
            h, u = _layer_ab(h, norm_mix[layer], ab_w_in[j], ab_conv_w[j], ab_conv_b[j], ab_gate_b[j],
                             ab_head_norm[j], ab_w_out[j], norm_ffn[layer])
        else:
            h, u = _layer_cd(h, norm_mix[layer], cd_w_in[j], cd_cmp_pos[j], cd_cmp_w1[j], cd_cmp_w2[j],
                             lb_all[layer], cd_head_norm[j], cd_w_out[j], norm_ffn[layer])
        h = _ffn(h, u, ffn_w_up[layer], ffn_conv_w[layer], ffn_conv_b[layer], ffn_w_down[layer])
    return _rmsnorm(h, norm_final.reshape(1, -1), tm=min(512, h.shape[1]))
```

```python
import functools

import numpy as np
import jax
import jax.numpy as jnp
from jax import lax
from jax.experimental import pallas as pl
from jax.experimental.pallas import tpu as pltpu

F32 = jnp.float32
BF16 = jnp.bfloat16

HEAD_DIM = 128
N_HEADS = 8
KV_GROUPS = 2
GROUP_HEADS = N_HEADS // KV_GROUPS
A_CONV = 4
CMP_LEN = 32
CMP_STRIDE = 16
SEL_LEN = 64
SEL_SHIFT = 6
N_SEL = 8
N_LOCAL = 2
FORCE_BONUS = 1000.0
WINDOW = 256
FFN_CONV = 3
ROPE_THETA = 500000.0
ROPE_DIM = HEAD_DIM // 4
RMS_EPS = 1e-6
SCALE = HEAD_DIM ** -0.5
LOG2_E = float(np.log2(np.e))

CHUNK = 128
HEADS_PER_STEP = 4
HGRN_HEADS_PER_STEP = 8
CD_PROJ_COL_TILES = 3
HGRN_LEVELS = (64, 32, 16, 8, 4, 2, 1)
FFN_COL_TILES = 4
OUT_ROW_TILE = 512
OUT_COLS = 512
SB_SUB = 256
SB_ROW_GROUPS = 4
SB_DEAD_LOG2 = -160.0
PROJ_COL_TILES = 4
MXU_COLS = 256
VMEM_LIMIT = 56 * 1024 * 1024


def _cparams(sem):
    return pltpu.CompilerParams(dimension_semantics=sem, vmem_limit_bytes=VMEM_LIMIT)


def _dot(a, b):
    return jnp.dot(a, b, preferred_element_type=F32)


def _dot_nt(a, b):
    return lax.dot_general(a, b, (((1,), (1,)), ((), ())), preferred_element_type=F32)


def _dot_split(a, b01, terms):
    acc = None
    rem = a
    for _ in range(terms):
        piece = rem.astype(BF16)
        part = _dot(piece, b01)
        acc = part if acc is None else acc + part
        rem = rem - piece.astype(F32)
    return acc


def _dot_split_rhs(a01, b, terms):
    acc = None
    rem = b
    for _ in range(terms):
        piece = rem.astype(BF16)
        part = _dot(a01, piece)
        acc = part if acc is None else acc + part
        rem = rem - piece.astype(F32)
    return acc


def _dot_nt_split(a01, b, terms):
    acc = None
    rem = b
    for _ in range(terms):
        piece = rem.astype(BF16)
        part = _dot_nt(a01, piece)
        acc = part if acc is None else acc + part
        rem = rem - piece.astype(F32)
    return acc


def _softplus(z):
    return jnp.maximum(z, 0.0) + jnp.log(1.0 + jnp.exp(-jnp.abs(z)))


def _log_sigmoid(z):
    return -_softplus(-z)


def _sigmoid(z):
    return 1.0 / (1.0 + jnp.exp(-z))


def _iota(shape, dim):
    return lax.broadcasted_iota(jnp.int32, shape, dim)


def _rope_swap(x):
    lane = _iota(x.shape, x.ndim - 1)
    up = pltpu.roll(x, HEAD_DIM - ROPE_DIM // 2, axis=x.ndim - 1)
    down = pltpu.roll(x, ROPE_DIM // 2, axis=x.ndim - 1)
    return jnp.where(lane < ROPE_DIM // 2, up, down)


def _rope_tables(pos):
    half = ROPE_DIM // 2
    freqs = ROPE_THETA ** (-np.arange(half, dtype=np.float32) / half)
    ang = jnp.asarray(pos, F32)[:, None] * jnp.asarray(freqs, F32)[None, :]
    cos, sin = jnp.cos(ang), jnp.sin(ang)
    n = ang.shape[0]
    cos_t = jnp.concatenate([cos, cos, jnp.ones((n, HEAD_DIM - ROPE_DIM), F32)], axis=1)
    sin_t = jnp.concatenate([-sin, sin, jnp.zeros((n, HEAD_DIM - ROPE_DIM), F32)], axis=1)
    return cos_t, sin_t


def _rms_to_bf16(h_ref, g_ref):
    x = h_ref[...]
    ms = jnp.mean(x * x, axis=-1, keepdims=True)
    return (x * lax.rsqrt(ms + RMS_EPS) * g_ref[...]).astype(BF16)


def _norm_proj_kernel(*refs, gates_in_main):
    if gates_in_main:
        h_ref, g_ref, w_ref, o_ref, og_ref, u_scr = refs
    else:
        h_ref, g_ref, w_ref, wg_ref, o_ref, og_ref, u_scr = refs
    j = pl.program_id(2)

    @pl.when(j == 0)
    def _():
        u = _rms_to_bf16(h_ref, g_ref)
        u_scr[...] = u
        if not gates_in_main:
            og_ref[...] = _dot(u, wg_ref[...])

    u = u_scr[...]
    tn = w_ref.shape[1]
    for c0 in range(0, tn, MXU_COLS):
        width = min(MXU_COLS, tn - c0)
        res = _dot(u, w_ref[:, c0:c0 + width])
        for c in range(width // HEAD_DIM):
            o_ref[c0 // HEAD_DIM + c] = res[:, c * HEAD_DIM:(c + 1) * HEAD_DIM].astype(o_ref.dtype)
        if gates_in_main and c0 + width == tn:
            @pl.when(j == pl.num_programs(2) - 1)
            def _():
                og_ref[...] = res


def _norm_proj(h, g, w, wg, *, tm, tn, gates_in_main=False):
    b, s, d = h.shape
    ng = wg.shape[1]
    if gates_in_main:
        assert ng == MXU_COLS
        w = jnp.concatenate([w, wg], axis=1)
    n = w.shape[1]
    ncb = tn // HEAD_DIM
    operands = [h, g, w] + ([] if gates_in_main else [wg])
    return pl.pallas_call(
        functools.partial(_norm_proj_kernel, gates_in_main=gates_in_main),
        grid=(b, s // tm, n // tn),
        in_specs=[
            pl.BlockSpec((None, tm, d), lambda bi, si, j: (bi, si, 0)),
            pl.BlockSpec((1, d), lambda bi, si, j: (0, 0)),
            pl.BlockSpec((d, tn), lambda bi, si, j: (0, j)),
        ] + ([] if gates_in_main else [pl.BlockSpec((d, ng), lambda bi, si, j: (0, 0))]),
        out_specs=[
            pl.BlockSpec((None, ncb, tm, HEAD_DIM), lambda bi, si, j: (bi, j, si, 0)),
            pl.BlockSpec((None, tm, ng), lambda bi, si, j: (bi, si, 0)),
        ],
        out_shape=[
            jax.ShapeDtypeStruct((b, n // HEAD_DIM, s, HEAD_DIM), BF16),
            jax.ShapeDtypeStruct((b, s, ng), F32),
        ],
        scratch_shapes=[pltpu.VMEM((tm, d), BF16)],
        compiler_params=_cparams(("parallel", "parallel", "arbitrary")),
        name="norm_proj",
    )(*operands)


def _ffn_up_kernel(u_ref, wg_ref, wu_ref, cw_ref, cb_ref, o_ref, halo_scr):
    si = pl.program_id(1)
    j = pl.program_id(2)
    tm, tn = o_ref.shape

    @pl.when(si == 0)
    def _():
        halo_scr[j] = jnp.zeros(halo_scr.shape[1:], F32)

    u = u_ref[...]
    cw = cw_ref[...]
    cb = cb_ref[...]
    for c0 in range(0, tn, HEAD_DIM):
        cols = slice(c0, c0 + HEAD_DIM)
        res = _dot(u, jnp.concatenate([wg_ref[:, cols], wu_ref[:, cols]], axis=1))
        gate = res[:, :HEAD_DIM]
        up = res[:, HEAD_DIM:]
        ext = jnp.concatenate([halo_scr[j, :, cols], gate], axis=0)
        halo_scr[j, :, cols] = gate[tm - 8:, :]
        conv = cb[:, cols] + cw[FFN_CONV - 1:FFN_CONV, cols] * gate
        for k in range(FFN_CONV - 1):
            off = 8 - (FFN_CONV - 1) + k
            conv = conv + cw[k:k + 1, cols] * ext[off:off + tm, :]
        o_ref[:, cols] = (conv * _sigmoid(conv) * up).astype(o_ref.dtype)


def _ffn_up(u, w_up, cw, cb, *, tm, tn):
    b, s, d = u.shape
    f = w_up.shape[1] // 2
    nj = f // tn
    return pl.pallas_call(
        _ffn_up_kernel,
        grid=(b, s // tm, nj),
        in_specs=[
            pl.BlockSpec((None, tm, d), lambda bi, si, j: (bi, si, 0)),
            pl.BlockSpec((d, tn), lambda bi, si, j: (0, j)),
            pl.BlockSpec((d, tn), lambda bi, si, j: (0, nj + j)),
            pl.BlockSpec((FFN_CONV, tn), lambda bi, si, j: (0, j)),
            pl.BlockSpec((1, tn), lambda bi, si, j: (0, j)),
        ],
        out_specs=pl.BlockSpec((None, tm, tn), lambda bi, si, j: (bi, si, j)),
        out_shape=jax.ShapeDtypeStruct((b, s, f), BF16),
        scratch_shapes=[pltpu.VMEM((f // tn, 8, tn), F32)],
        compiler_params=_cparams(("parallel", "arbitrary", "arbitrary")),
        name="ffn_up",
    )(u, w_up, w_up, cw, cb)


def _out_proj_norm_kernel(h_ref, a1_ref, w1_ref, a2_ref, w2_ref, g_ref, ho_ref, u_ref):
    tm, d = ho_ref.shape
    a1 = a1_ref[...]
    a2 = a2_ref[...]
    ss = jnp.zeros((tm, 1), F32)
    for c0 in range(0, d, OUT_COLS):
        cols = slice(c0, c0 + OUT_COLS)
        acc = h_ref[:, cols] + _dot(a1, w1_ref[:, cols]) + _dot(a2, w2_ref[:, cols])
        ho_ref[:, cols] = acc
        ss = ss + jnp.sum(acc * acc, axis=1, keepdims=True)
    inv = lax.rsqrt(ss * (1.0 / d) + RMS_EPS)
    for c0 in range(0, d, OUT_COLS):
        cols = slice(c0, c0 + OUT_COLS)
        u_ref[:, cols] = (ho_ref[:, cols] * inv * g_ref[:, cols]).astype(u_ref.dtype)


def _out_proj_norm(h, a1, a2, w, g, *, tm):
    b, s, d = h.shape
    k = a1.shape[-1]
    assert a2.shape[-1] == k and w.shape[0] == 2 * k
    rows = lambda n: pl.BlockSpec((None, tm, n), lambda bi, si: (bi, si, 0))
    half = lambda i: pl.BlockSpec((k, d), lambda bi, si: (i, 0))
    return pl.pallas_call(
        _out_proj_norm_kernel,
        grid=(b, s // tm),
        in_specs=[rows(d), rows(k), half(0), rows(k), half(1), pl.BlockSpec((1, d), lambda bi, si: (0, 0))],
        out_specs=[rows(d), rows(d)],
        out_shape=[jax.ShapeDtypeStruct((b, s, d), F32), jax.ShapeDtypeStruct((b, s, d), BF16)],
        compiler_params=_cparams(("parallel", "parallel")),
        name="out_proj_norm",
    )(h, a1, w, a2, w, g)


def _proj_res_kernel(*refs):
    h_ref, o_ref = refs[0], refs[-1]
    ops = refs[1:-1]
    acc = h_ref[...]
    for i in range(0, len(ops), 2):
        acc = acc + _dot(ops[i][...], ops[i + 1][...])
    o_ref[...] = acc


def _proj_res(h, pairs, *, tm, tn):
    b, s, d = h.shape
    in_specs = [pl.BlockSpec((None, tm, tn), lambda bi, si, j: (bi, si, j))]
    args = [h]
    for a, w in pairs:
        k = a.shape[-1]
        in_specs.append(pl.BlockSpec((None, tm, k), lambda bi, si, j: (bi, si, 0)))
        in_specs.append(pl.BlockSpec((k, tn), lambda bi, si, j: (0, j)))
        args += [a, w]
    return pl.pallas_call(
        _proj_res_kernel,
        grid=(b, s // tm, d // tn),
        in_specs=in_specs,
        out_specs=pl.BlockSpec((None, tm, tn), lambda bi, si, j: (bi, si, j)),
        out_shape=jax.ShapeDtypeStruct((b, s, d), F32),
        compiler_params=_cparams(("parallel", "parallel", "arbitrary")),
        name="proj_res",
    )(*args)


def _rmsnorm_kernel(h_ref, g_ref, o_ref):
    x = h_ref[...]
    ms = jnp.mean(x * x, axis=-1, keepdims=True)
    o_ref[...] = x * lax.rsqrt(ms + RMS_EPS) * g_ref[...]


def _rmsnorm(h, g, *, tm):
    b, s, d = h.shape
    return pl.pallas_call(
        _rmsnorm_kernel,
        grid=(b, s // tm),
        in_specs=[pl.BlockSpec((None, tm, d), lambda bi, si: (bi, si, 0)),
                  pl.BlockSpec((1, d), lambda bi, si: (0, 0))],
        out_specs=pl.BlockSpec((None, tm, d), lambda bi, si: (bi, si, 0)),
        out_shape=jax.ShapeDtypeStruct((b, s, d), F32),
        compiler_params=_cparams(("parallel", "parallel")),
        name="final_norm",
    )(h, g)


def _mlstm_kernel(q_ref, k_ref, v_ref, og_ref, ig_ref, fg_ref, bi_ref, bf_ref,
                  cwq_ref, cwk_ref, cbq_ref, cbk_ref, hn_ref, out_ref,
                  pad_scr, qc_scr, kc_scr, c_scr, b_scr, d_scr):
    nh, s, _ = q_ref.shape
    nch = s // CHUNK

    def conv_silu(src_ref, cw_ref, cb_ref, dst_ref, hh, scale):
        lanes = slice(hh * HEAD_DIM, (hh + 1) * HEAD_DIM)
        pad_scr[0:8, :] = jnp.zeros((8, HEAD_DIM), F32)
        for p in range(nch):
            pad_scr[8 + p * CHUNK:8 + (p + 1) * CHUNK, :] = src_ref[hh, p * CHUNK:(p + 1) * CHUNK, :].astype(F32)
        cw = cw_ref[:, lanes]
        for p in range(nch):
            acc = cb_ref[:, lanes] + cw[0:1, :] * pad_scr[pl.ds(8 - (A_CONV - 1) + p * CHUNK, CHUNK), :]
            for j in range(1, A_CONV):
                acc = acc + cw[j:j + 1, :] * pad_scr[pl.ds(8 - (A_CONV - 1) + j + p * CHUNK, CHUNK), :]
            y = acc * _sigmoid(acc)
            if scale != 1.0:
                y = y * scale
            dst_ref[hh, p * CHUNK:(p + 1) * CHUNK, :] = y.astype(BF16)

    r_i = _iota((CHUNK, CHUNK), 0)
    c_i = _iota((CHUNK, CHUNK), 1)
    upper = (r_i <= c_i).astype(BF16)
    for hh in range(nh):
        conv_silu(q_ref, cwq_ref, cbq_ref, qc_scr, hh, 1.0)
        conv_silu(k_ref, cwk_ref, cbk_ref, kc_scr, hh, SCALE)
        log_f = _log_sigmoid(fg_ref[hh] + bf_ref[hh])
        log_i = ig_ref[hh] + bi_ref[hh]
        b_all = _dot_split(log_f, upper, 3)
        b_scr[hh] = b_all
        d_scr[hh] = log_i - b_all
    c_scr[...] = jnp.zeros(c_scr.shape, F32)
    causal = c_i <= r_i
    ones_blk = jnp.ones((CHUNK, HEAD_DIM), BF16)

    def head_chunk(hh, c, t0, m):
        lanes = slice(hh * HEAD_DIM, (hh + 1) * HEAD_DIM)
        q = qc_scr[hh, pl.ds(t0, CHUNK), :]
        k = kc_scr[hh, pl.ds(t0, CHUNK), :]
        v_aug = jnp.concatenate([v_ref[hh, pl.ds(t0, CHUNK), :], ones_blk], axis=1)
        b_row = b_scr[hh, pl.ds(c, 1), :]
        d_row = d_scr[hh, pl.ds(c, 1), :]
        b_col = jnp.transpose(jnp.broadcast_to(b_row, (CHUNK, CHUNK)))
        d_col = jnp.transpose(jnp.broadcast_to(d_row, (CHUNK, CHUNK)))
        dmat = jnp.where(causal, b_col + d_row, -jnp.inf)
        g = b_col + m
        m_row = jnp.maximum(g, jnp.max(dmat, axis=1, keepdims=True))
        w = jnp.exp(dmat - m_row) * _dot_nt(q, k)
        w_state = jnp.exp(g - m_row)
        c_old = c_scr[hh]
        inter = _dot(q, c_old.astype(BF16))
        intra = _dot(w.astype(BF16), v_aug)
        tot = jnp.concatenate([w_state, w_state], axis=1) * inter + intra
        num = tot[:, :HEAD_DIM]
        den = tot[:, HEAD_DIM:]
        hv = num / jnp.maximum(jnp.abs(den), jnp.exp(-m_row))
        y = hv * lax.rsqrt(jnp.mean(hv * hv, axis=-1, keepdims=True) + RMS_EPS) * hn_ref[:, lanes]
        y = y * _sigmoid(og_ref[hh, pl.ds(t0, CHUNK), :].astype(F32))
        out_ref[pl.ds(t0, CHUNK), lanes] = y.astype(out_ref.dtype)
        b_last = b_col[CHUNK - 1:CHUNK, :]
        w_new = b_last + d_col
        m_new = jnp.maximum(b_last + m, jnp.max(w_new, axis=0, keepdims=True))
        decay = jnp.exp(b_last + m - m_new)
        kw = jnp.exp(w_new - m_new) * k.astype(F32)
        kw_t = jnp.transpose(kw).astype(BF16)
        c_scr[hh] = jnp.concatenate([decay, decay], axis=1) * c_old + _dot(kw_t, v_aug)
        return m_new

    def body(c, ms):
        t0 = pl.multiple_of(c * CHUNK, CHUNK)
        return tuple(head_chunk(hh, c, t0, ms[hh]) for hh in range(nh))

    lax.fori_loop(0, nch, body, tuple(jnp.zeros((1, CHUNK), F32) for _ in range(nh)))


def _mlstm(p, gi, gf, bias_i, bias_f, cw, cb, hn):
    b, _, s, _ = p.shape
    nch = s // CHUNK
    nh = HEADS_PER_STEP
    nblk = N_HEADS // nh
    w = nh * HEAD_DIM
    slab = lambda off: pl.BlockSpec((None, nh, s, HEAD_DIM), lambda bi, hi: (bi, off + hi, 0, 0))
    gate = pl.BlockSpec((None, nh, nch, CHUNK), lambda bi, hi: (bi, hi, 0, 0))
    hb = pl.BlockSpec((nh, 1, CHUNK), lambda bi, hi: (hi, 0, 0))
    return pl.pallas_call(
        _mlstm_kernel,
        grid=(b, nblk),
        in_specs=[
            slab(0), slab(nblk), slab(2 * nblk), slab(3 * nblk), gate, gate, hb, hb,
            pl.BlockSpec((A_CONV, w), lambda bi, hi: (0, hi)),
            pl.BlockSpec((A_CONV, w), lambda bi, hi: (0, nblk + hi)),
            pl.BlockSpec((1, w), lambda bi, hi: (0, hi)),
            pl.BlockSpec((1, w), lambda bi, hi: (0, nblk + hi)),
            pl.BlockSpec((1, w), lambda bi, hi: (0, hi)),
        ],
        out_specs=pl.BlockSpec((None, s, w), lambda bi, hi: (bi, 0, hi)),
        out_shape=jax.ShapeDtypeStruct((b, s, N_HEADS * HEAD_DIM), BF16),
        scratch_shapes=[
            pltpu.VMEM((s + 8, HEAD_DIM), F32),
            pltpu.VMEM((nh, s, HEAD_DIM), BF16),
            pltpu.VMEM((nh, s, HEAD_DIM), BF16),
            pltpu.VMEM((nh, HEAD_DIM, 2 * HEAD_DIM), F32),
            pltpu.VMEM((nh, nch, CHUNK), F32),
            pltpu.VMEM((nh, nch, CHUNK), F32),
        ],
        compiler_params=_cparams(("parallel", "parallel")),
        name="mlstm",
    )(p, p, p, p, gi, gf, bias_i, bias_f, cw, cw, cb, cb, hn)


def _sb_kernel(q_ref, k_ref, v_ref, o_ref):
    qi = pl.program_id(2)
    n_rg = SB_ROW_GROUPS
    sub = q_ref.shape[0] // n_rg
    r_i = _iota((sub, sub), 0)
    c_i = _iota((sub, sub), 1)
    suffix = (r_i > c_i).astype(BF16)
    strict = c_i < r_i
    qs = [q_ref[g * sub:(g + 1) * sub, :] for g in range(n_rg)]
    j_diag = [n_rg * qi + g for g in range(n_rg)]

    def sub_block(q, j, run, diag):
        k0 = pl.multiple_of(j * sub, sub)
        k = k_ref[pl.ds(k0, sub), :]
        v = v_ref[pl.ds(k0, sub), :]
        z2 = _dot_nt(q, k)
        sp2 = jnp.maximum(z2, 0.0) + jnp.log2(1.0 + jnp.exp2(-jnp.abs(z2)))
        log_keep = -sp2
        if diag:
            log_keep = jnp.where(strict, log_keep, 0.0)
        later = _dot(log_keep.astype(BF16), suffix) + run
        a = jnp.exp2(z2 - sp2 + later)
        if diag:
            a = jnp.where(strict, a, 0.0)
        return _dot(a.astype(BF16), v), jnp.sum(log_keep, axis=1, keepdims=True)

    def any_alive(runs, n_next):
        top = jnp.full((sub, 1), -jnp.inf, F32)
        for g in range(n_rg):
            top = jnp.maximum(top, jnp.where(j_diag[g] >= n_next, runs[g], -jnp.inf))
        return (jnp.max(top) > SB_DEAD_LOG2).astype(jnp.int32)

    accs, runs = [], []
    for g in range(n_rg):
        o, sm = sub_block(qs[g], j_diag[g], 0.0, True)
        accs.append(o)
        runs.append(sm)

    def cond(carry):
        return carry[1] > 0

    def body(carry):
        n = carry[0]
        accs, runs = list(carry[2]), list(carry[3])
        for g in range(n_rg):
            j = j_diag[g] - n
            o, sm = sub_block(qs[g], jnp.maximum(j, 0), runs[g], False)
            live = j >= 0
            accs[g] = accs[g] + jnp.where(live, o, 0.0)
            runs[g] = runs[g] + jnp.where(live, sm, 0.0)
        return n + 1, any_alive(runs, n + 1), tuple(accs), tuple(runs)

    out = lax.while_loop(cond, body, (jnp.int32(1), any_alive(runs, 1), tuple(accs), tuple(runs)))
    for g in range(n_rg):
        o_ref[g * sub:(g + 1) * sub, :] = out[2][g].astype(o_ref.dtype)


def _stick_breaking(p, *, q_off, k_off, v_off, tq):
    b, _, s, _ = p.shape
    return pl.pallas_call(
        _sb_kernel,
        grid=(b, N_HEADS, s // tq),
        in_specs=[
            pl.BlockSpec((None, None, tq, HEAD_DIM), lambda bi, hi, qi: (bi, q_off + hi, qi, 0)),
            pl.BlockSpec((None, None, s, HEAD_DIM), lambda bi, hi, qi: (bi, k_off + hi, 0, 0)),
            pl.BlockSpec((None, None, s, HEAD_DIM), lambda bi, hi, qi: (bi, v_off + hi, 0, 0)),
        ],
        out_specs=pl.BlockSpec((None, tq, HEAD_DIM), lambda bi, hi, qi: (bi, qi, hi)),
        out_shape=jax.ShapeDtypeStruct((b, s, N_HEADS * HEAD_DIM), BF16),
        compiler_params=_cparams(("parallel", "parallel", "arbitrary")),
        name="stick_breaking",
    )(p, p, p)


def _gelu_tanh(x):
    return 0.5 * x * (1.0 + jnp.tanh(np.sqrt(2.0 / np.pi).astype(np.float32) * (x + 0.044715 * (x * x * x))))


def _nsa_prep_kernel(kcs_ref, vcs_ref, ks_ref, kw_ref, pos_ref, w1_ref, w2_ref,
                     cos_ref, sin_ref, cosc_ref, sinc_ref,
                     kc_out, vc_out, ks_out, kw_out, f32_scr):
    s = ks_ref.shape[0]
    ng = s // CMP_STRIDE

    def compress(src_ref, idx):
        f32_scr[...] = src_ref[...].astype(F32)
        xa = jnp.zeros((ng, HEAD_DIM), F32)
        xb = jnp.zeros((ng, HEAD_DIM), F32)
        for l in range(CMP_STRIDE):
            rows = f32_scr[pl.ds(l, ng, stride=CMP_STRIDE), :]
            xa = xa + _dot((rows + pos_ref[idx, l:l + 1, :]).astype(BF16), w1_ref[idx, l])
            xb = xb + _dot((rows + pos_ref[idx, CMP_STRIDE + l:CMP_STRIDE + l + 1, :]).astype(BF16),
                           w1_ref[idx, CMP_STRIDE + l])
        hid = _gelu_tanh(xa + pltpu.roll(xb, ng - 1, axis=0))
        return _dot(hid.astype(BF16), w2_ref[idx])

    kc = compress(kcs_ref, 0)
    kc_out[...] = (kc * cosc_ref[...] + _rope_swap(kc) * sinc_ref[...]).astype(kc_out.dtype)
    vc_out[...] = compress(vcs_ref, 1).astype(vc_out.dtype)
    for src, dst in ((ks_ref, ks_out), (kw_ref, kw_out)):
        x = src[...].astype(F32)
        dst[...] = (x * cos_ref[...] + _rope_swap(x) * sin_ref[...]).astype(dst.dtype)


def _nsa_prep(p, pos, w1, w2, cos, sin, cosc, sinc, *, kc_off, vc_off, ks_off, kw_off):
    b, _, s, _ = p.shape
    ng = s // CMP_STRIDE
    slab = lambda off: pl.BlockSpec((None, None, s, HEAD_DIM), lambda bi, gi: (bi, off + gi, 0, 0))
    full = lambda a: pl.BlockSpec(a.shape, lambda bi, gi: (0,) * a.ndim)
    out_c = pl.BlockSpec((None, None, ng, HEAD_DIM), lambda bi, gi: (bi, gi, 0, 0))
    out_s = pl.BlockSpec((None, None, s, HEAD_DIM), lambda bi, gi: (bi, gi, 0, 0))
    return pl.pallas_call(
        _nsa_prep_kernel,
        grid=(b, KV_GROUPS),
        in_specs=[slab(kc_off), slab(vc_off), slab(ks_off), slab(kw_off),
                  full(pos), full(w1), full(w2), full(cos), full(sin), full(cosc), full(sinc)],
        out_specs=[out_c, out_c, out_s, out_s],
        out_shape=[jax.ShapeDtypeStruct((b, KV_GROUPS, ng, HEAD_DIM), BF16),
                   jax.ShapeDtypeStruct((b, KV_GROUPS, ng, HEAD_DIM), BF16),
                   jax.ShapeDtypeStruct((b, KV_GROUPS, s, HEAD_DIM), BF16),
                   jax.ShapeDtypeStruct((b, KV_GROUPS, s, HEAD_DIM), BF16)],
        scratch_shapes=[pltpu.VMEM((s, HEAD_DIM), F32)],
        compiler_params=_cparams(("parallel", "parallel")),
        name="nsa_prep",
    )(p, p, p, p, pos, w1, w2, cos, sin, cosc, sinc)


def _masked_softmax(sc, mask):
    sc = jnp.where(mask, sc, -jnp.inf)
    m = jnp.max(sc, axis=-1, keepdims=True)
    m = jnp.where(m > -jnp.inf, m, 0.0)
    e = jnp.exp(sc - m)
    return e / jnp.maximum(jnp.sum(e, axis=-1, keepdims=True), 1e-30)


def _nsa_kernel(q_ref, kc_ref, vc_ref, ks_ref, vs_ref, kw_ref, vw_ref, gate_ref, cos_ref, sin_ref,
                cover_t_ref, o_ref, *, tk):
    qi = pl.program_id(2)
    r = q_ref.shape[0]
    tq = q_ref.shape[1]
    s_len = ks_ref.shape[0]
    n_cmp = (s_len - CMP_LEN) // CMP_STRIDE + 1
    t0 = qi * tq

    qf = q_ref[...].astype(F32).reshape(r * tq, HEAD_DIM)
    cos = jnp.concatenate([cos_ref[...]] * r, axis=0)
    sin = jnp.concatenate([sin_ref[...]] * r, axis=0)
    qb = (qf * cos + _rope_swap(qf) * sin).astype(BF16)

    t_col = t0 + _iota((tq, HEAD_DIM), 0)
    lane = _iota((tq, HEAD_DIM), 1)

    ncl = kc_ref.shape[0]
    sc = _dot_nt(qb, kc_ref[...]).reshape(r, tq, ncl)
    cmp_ok = ((lane * CMP_STRIDE + (CMP_LEN - 1) <= t_col) & (lane < n_cmp))[:, :ncl]
    p_cmp = _masked_softmax(sc, cmp_ok[None])
    o_cmp = _dot(p_cmp.reshape(r * tq, ncl).astype(BF16), vc_ref[...]).reshape(r, tq, HEAD_DIM)

    band = WINDOW + tq
    w0 = pl.multiple_of(jnp.maximum(t0 - WINDOW, 0), tq)
    k_win = kw_ref[pl.ds(w0, band), :]
    v_win = vw_ref[pl.ds(w0, band), :]
    diff = (t0 + _iota((tq, band), 0)) - (w0 + _iota((tq, band), 1))
    win_ok = (diff >= 0) & (diff < WINDOW)
    sw = _dot_nt(qb, k_win).reshape(r, tq, band)
    p_win = _masked_softmax(sw, win_ok[None])
    o_win = _dot(p_win.reshape(r * tq, band).astype(BF16), v_win).reshape(r, tq, HEAD_DIM)

    p_sum = p_cmp[0]
    for h in range(1, r):
        p_sum = p_sum + p_cmp[h]
    n_blk = s_len // SEL_LEN
    p_slc_t = _dot_nt_split(cover_t_ref[...], p_sum, 3)[:n_blk, :]
    blk = _iota((n_blk, tq), 0)
    blk_t = (t0 + _iota((n_blk, tq), 1)) >> SEL_SHIFT
    started = blk <= blk_t
    forced = (blk == 0) | (started & (blk_t - blk < N_LOCAL))
    score = jnp.where(started, p_slc_t + jnp.where(forced, FORCE_BONUS, 0.0), -jnp.inf)
    rank = jnp.zeros((n_blk, tq), F32)
    for jp in range(n_blk):
        row = score[jp:jp + 1, :]
        beats = (row > score) | ((row == score) & (blk > jp))
        rank = rank + jnp.where(beats, 1.0, 0.0)
    sel_t = jnp.where((rank < N_SEL) & started, 1.0, 0.0)
    sel_t = jnp.concatenate([sel_t, jnp.zeros((HEAD_DIM - n_blk, tq), F32)], axis=0)
    sel = jnp.transpose(sel_t).astype(BF16)

    e_row = _iota((HEAD_DIM, tk), 0)
    e_blk = _iota((HEAD_DIM, tk), 1) >> SEL_SHIFT
    s_loc = _iota((tq, tk), 1)
    t_q = t0 + _iota((tq, tk), 0)
    n_chain = r
    hc = r // n_chain
    q_chain = [qb[c * hc * tq:(c + 1) * hc * tq, :] for c in range(n_chain)]
    ones_blk = jnp.ones((tk, HEAD_DIM), BF16)

    def sel_tile(kt, carry):
        k0 = pl.multiple_of(kt * tk, tk)
        k = ks_ref[pl.ds(k0, tk), :]
        v_aug = jnp.concatenate([vs_ref[pl.ds(k0, tk), :], ones_blk], axis=1)
        expand = (e_row == e_blk + kt * (tk // SEL_LEN)).astype(BF16)
        ok = (_dot(sel, expand) > 0.5) & (k0 + s_loc <= t_q)
        bias = jnp.where(ok, 0.0, -jnp.inf)[None]
        new = []
        for c in range(n_chain):
            m, acc = carry[c]
            sc_t = _dot_nt(q_chain[c], k).reshape(hc, tq, tk) + bias
            m_new = jnp.maximum(m, jnp.max(sc_t, axis=-1, keepdims=True))
            alpha = jnp.exp(m - m_new)
            e = jnp.exp(sc_t - m_new)
            pv = _dot(e.reshape(hc * tq, tk).astype(BF16), v_aug).reshape(hc, tq, 2 * HEAD_DIM)
            new.append((m_new, alpha * acc + pv))
        return tuple(new)

    n_tiles = (t0 + tq + tk - 1) // tk
    init = tuple((jnp.full((hc, tq, 1), -jnp.inf, F32), jnp.zeros((hc, tq, 2 * HEAD_DIM), F32))
                 for _ in range(n_chain))
    chains = lax.fori_loop(0, n_tiles, sel_tile, init)

    gates = _sigmoid(gate_ref[...])
    for h in range(r):
        acc_s = chains[h // hc][1][h % hc]
        o_slc = acc_s[:, :HEAD_DIM] / jnp.maximum(acc_s[:, HEAD_DIM:], 1e-30)
        out = (gates[:, 3 * h:3 * h + 1] * o_cmp[h] + gates[:, 3 * h + 1:3 * h + 2] * o_slc
               + gates[:, 3 * h + 2:3 * h + 3] * o_win[h])
        o_ref[:, h * HEAD_DIM:(h + 1) * HEAD_DIM] = out.astype(o_ref.dtype)


def _nsa(p, kc, vc, ks, kw, gates, cos, sin, cover, *, q_off, vs_off, vw_off, tq, tk):
    b, _, s, _ = p.shape
    r = GROUP_HEADS
    ng = kc.shape[2]
    kv = lambda: pl.BlockSpec((None, None, s, HEAD_DIM), lambda bi, gi, qi: (bi, gi, 0, 0))
    pslab = lambda off: pl.BlockSpec((None, None, s, HEAD_DIM), lambda bi, gi, qi: (bi, off + gi, 0, 0))
    cmp_spec = pl.BlockSpec((None, None, ng, HEAD_DIM), lambda bi, gi, qi: (bi, gi, 0, 0))
    return pl.pallas_call(
        functools.partial(_nsa_kernel, tk=tk),
        grid=(b, KV_GROUPS, s // tq),
        in_specs=[
            pl.BlockSpec((None, r, tq, HEAD_DIM), lambda bi, gi, qi: (bi, q_off // r + gi, qi, 0)),
            cmp_spec, cmp_spec, kv(), pslab(vs_off), kv(), pslab(vw_off),
            pl.BlockSpec((None, tq, HEAD_DIM), lambda bi, gi, qi: (bi, qi, gi)),
            pl.BlockSpec((tq, HEAD_DIM), lambda bi, gi, qi: (qi, 0)),
            pl.BlockSpec((tq, HEAD_DIM), lambda bi, gi, qi: (qi, 0)),
            pl.BlockSpec(cover.shape, lambda bi, gi, qi: (0, 0)),
        ],
        out_specs=pl.BlockSpec((None, tq, r * HEAD_DIM), lambda bi, gi, qi: (bi, qi, gi)),
        out_shape=jax.ShapeDtypeStruct((b, s, N_HEADS * HEAD_DIM), BF16),
        compiler_params=_cparams(("parallel", "parallel", "arbitrary")),
        name="nsa",
    )(p, kc, vc, ks, p, kw, p, gates, cos, sin, cover)


def _hgrn_tables():
    r = np.arange(CHUNK)[:, None]
    j = np.arange(CHUNK)[None, :]
    mats = []
    for sz in HGRN_LEVELS:
        mid = (r // (2 * sz)) * (2 * sz) + sz
        right = (r // sz) % 2 == 1
        mats.append(np.where(right, (j >= mid) & (j <= r), (j > r) & (j <= mid - 1)))
    mats.append(j <= r)
    mats.append(j > r)
    return np.concatenate(mats, axis=0).astype(np.float32)


def _hgrn_kernel(q_ref, f_ref, i_ref, g_ref, lb_ref, hn_ref, tab_ref, out_ref, st_scr):
    nh, s, _ = q_ref.shape
    nch = s // CHUNK
    nlev = len(HGRN_LEVELS)
    r_i = _iota((CHUNK, CHUNK), 0)
    c_i = _iota((CHUNK, CHUNK), 1)
    row = _iota((CHUNK, HEAD_DIM), 0)
    st_scr[...] = jnp.zeros(st_scr.shape, F32)

    def head_chunk(hh, t0):
        lanes = slice(hh * HEAD_DIM, (hh + 1) * HEAD_DIM)
        lb = lb_ref[:, lanes]
        q = q_ref[hh, pl.ds(t0, CHUNK), :].astype(F32)
        fp = f_ref[hh, pl.ds(t0, CHUNK), :].astype(F32)
        v = i_ref[hh, pl.ds(t0, CHUNK), :]
        log_lb = jnp.log(lb)
        a2 = jnp.log1p(-lb) - _softplus(-fp)
        hi = jnp.maximum(log_lb, a2)
        lo = jnp.minimum(log_lb, a2)
        log_f = hi + jnp.log(1.0 + jnp.exp(lo - hi))
        k = (1.0 - lb) * _sigmoid(-fp)
        x_all = _dot_split_rhs(tab_ref[...], log_f, 2)
        att = jnp.where(r_i == c_i, jnp.sum(q * k, axis=1, keepdims=True), 0.0)
        for lv, sz in enumerate(HGRN_LEVELS):
            sh = sz.bit_length() - 1
            e = jnp.exp(x_all[lv * CHUNK:(lv + 1) * CHUNK, :])
            right = ((row >> sh) & 1) == 1
            qt = jnp.where(right, q * e, 0.0).astype(BF16)
            kt = jnp.where(right, 0.0, k * e).astype(BF16)
            same = (r_i >> (sh + 1)) == (c_i >> (sh + 1))
            att = att + jnp.where(same, _dot_nt(qt, kt), 0.0)
        bc = x_all[nlev * CHUNK:(nlev + 1) * CHUNK, :]
        rev = x_all[(nlev + 1) * CHUNK:(nlev + 2) * CHUNK, :]
        st = st_scr[hh]
        o = _dot(att.astype(BF16), v) + _dot_nt((q * jnp.exp(bc)).astype(BF16), st.astype(BF16))
        y = o * lax.rsqrt(jnp.mean(o * o, axis=-1, keepdims=True) + RMS_EPS) * hn_ref[:, lanes]
        gg = g_ref[hh, pl.ds(t0, CHUNK), :].astype(F32)
        out_ref[pl.ds(t0, CHUNK), lanes] = (y * (gg * _sigmoid(gg))).astype(out_ref.dtype)
        v_t = jnp.transpose(v.astype(F32)).astype(BF16)
        st_scr[hh] = st * jnp.exp(bc[CHUNK - 1:CHUNK, :]) + _dot(v_t, (k * jnp.exp(rev)).astype(BF16))

    def body(c, carry):
        t0 = pl.multiple_of(c * CHUNK, CHUNK)
        for hh in range(nh):
            head_chunk(hh, t0)
        return carry

    lax.fori_loop(0, nch, body, 0)


def _hgrn(p, lb, hn, tab, *, q_off, f_off, i_off, g_off):
    b, _, s, _ = p.shape
    nh = HGRN_HEADS_PER_STEP
    nblk = N_HEADS // nh
    w = nh * HEAD_DIM
    slab = lambda off: pl.BlockSpec((None, nh, s, HEAD_DIM), lambda bi, hi: (bi, off // nh + hi, 0, 0))
    return pl.pallas_call(
        _hgrn_kernel,
        grid=(b, nblk),
        in_specs=[slab(q_off), slab(f_off), slab(i_off), slab(g_off),
                  pl.BlockSpec((1, w), lambda bi, hi: (0, hi)),
                  pl.BlockSpec((1, w), lambda bi, hi: (0, hi)),
                  pl.BlockSpec(tab.shape, lambda bi, hi: (0, 0))],
        out_specs=pl.BlockSpec((None, s, w), lambda bi, hi: (bi, 0, hi)),
        out_shape=jax.ShapeDtypeStruct((b, s, N_HEADS * HEAD_DIM), BF16),
        scratch_shapes=[pltpu.VMEM((nh, HEAD_DIM, HEAD_DIM), F32)],
        compiler_params=_cparams(("parallel", "parallel")),
        name="hgrn2",
    )(p, p, p, p, lb, hn, tab)


def _pad_cols(w, n):
    return jnp.pad(w, ((0, 0), (0, n - w.shape[1])))


def _row_tile(s):
    return min(1024, s)


def _layer_ab(h, norm_g, w_in, conv_w, conv_b, gate_b, head_norm, w_out, ffn_norm_g):
    b, s, d = h.shape
    aw = N_HEADS * HEAD_DIM
    tm = _row_tile(s)
    g0 = 4 * aw
    g1 = g0 + 2 * N_HEADS
    w_main = jnp.concatenate([w_in[:, :g0], w_in[:, g1:g1 + aw] * (SCALE * LOG2_E), w_in[:, g1 + aw:]],
                             axis=1).astype(BF16)
    w_gate = _pad_cols(w_in[:, g0:g1], HEAD_DIM).astype(BF16)
    p, gates = _norm_proj(h, norm_g.reshape(1, d), w_main, w_gate, tm=tm, tn=w_main.shape[1] // PROJ_COL_TILES)
    nch = s // CHUNK
    g_t = jnp.transpose(gates[:, :, :2 * N_HEADS], (0, 2, 1))
    gi = g_t[:, :N_HEADS].reshape(b, N_HEADS, nch, CHUNK)
    gf = g_t[:, N_HEADS:].reshape(b, N_HEADS, nch, CHUNK)
    bias_i = jnp.broadcast_to(gate_b[:N_HEADS, None, None], (N_HEADS, 1, CHUNK)).astype(F32)
    bias_f = jnp.broadcast_to(gate_b[N_HEADS:, None, None], (N_HEADS, 1, CHUNK)).astype(F32)
    h_a = _mlstm(p, gi, gf, bias_i, bias_f, conv_w, conv_b.reshape(1, -1), head_norm.reshape(1, -1))
    h_b = _stick_breaking(p, q_off=4 * N_HEADS, k_off=5 * N_HEADS, v_off=6 * N_HEADS, tq=min(SB_ROW_GROUPS * SB_SUB, s))
    w_o = w_out.astype(BF16)
    return _out_proj_norm(h, h_a, h_b, w_o, ffn_norm_g.reshape(1, d), tm=min(OUT_ROW_TILE, s))


def _layer_cd(h, norm_g, w_in, cmp_pos, cmp_w1, cmp_w2, lower_bound, head_norm, w_out, ffn_norm_g):
    b, s, d = h.shape
    cw = N_HEADS * HEAD_DIM
    kvw = KV_GROUPS * HEAD_DIM
    tm = _row_tile(s)
    n_gate = 3 * N_HEADS
    g0 = cw + 6 * kvw
    w_main = jnp.concatenate([w_in[:, g0 + n_gate:], w_in[:, :cw] * SCALE, w_in[:, cw:g0]], axis=1).astype(BF16)
    per_group = 3 * GROUP_HEADS
    w_gate = jnp.concatenate(
        [_pad_cols(w_in[:, g0 + g * per_group:g0 + (g + 1) * per_group], HEAD_DIM) for g in range(KV_GROUPS)],
        axis=1).astype(BF16)
    p, gates = _norm_proj(h, norm_g.reshape(1, d), w_main, w_gate, tm=tm, gates_in_main=True,
                          tn=(w_main.shape[1] + w_gate.shape[1]) // CD_PROJ_COL_TILES)
    pos = cmp_pos.astype(F32)
    w1 = cmp_w1.astype(BF16)
    w2 = cmp_w2.astype(BF16)
    cos, sin = _rope_tables(np.arange(s))
    ng = s // CMP_STRIDE
    cosc, sinc = _rope_tables(np.arange(ng) * CMP_STRIDE + CMP_LEN - 1)
    kc, vc, ks, kw = _nsa_prep(p, pos, w1, w2, cos, sin, cosc, sinc,
                               kc_off=40, vc_off=42, ks_off=44, kw_off=48)
    n_cmp = (s - CMP_LEN) // CMP_STRIDE + 1
    n_sel = s // SEL_LEN
    cs = np.arange(n_cmp) * CMP_STRIDE
    ss = np.arange(n_sel) * SEL_LEN
    cover_t = np.zeros((HEAD_DIM, ng), np.float32)
    cover_t[:n_sel, :n_cmp] = (cs[None, :] < ss[:, None] + SEL_LEN) & (cs[None, :] + CMP_LEN > ss[:, None])
    o_c = _nsa(p, kc, vc, ks, kw, gates, cos, sin, jnp.asarray(cover_t, BF16),
               q_off=32, vs_off=46, vw_off=50, tq=256, tk=min(512, s))
    o_d = _hgrn(p, lower_bound.reshape(1, -1).astype(F32), head_norm.reshape(1, -1),
                jnp.asarray(_hgrn_tables(), BF16), q_off=0, f_off=8, i_off=16, g_off=24)
    w_o = w_out.astype(BF16)
    return _out_proj_norm(h, o_c, o_d, w_o, ffn_norm_g.reshape(1, d), tm=min(OUT_ROW_TILE, s))


def _ffn(h, u, w_up, conv_w, conv_b, w_down):
    b, s, d = h.shape
    f = w_down.shape[0]
    tm = _row_tile(s)
    act = _ffn_up(u, w_up.astype(BF16), conv_w, conv_b.reshape(1, f), tm=tm, tn=f // FFN_COL_TILES)
    return _proj_res(h, [(act, w_down.astype(BF16))], tm=tm, tn=512)


def kernel(x, norm_mix, norm_ffn, norm_final, ab_w_in, ab_conv_w, ab_conv_b, ab_gate_b, ab_head_norm, ab_w_out, cd_w_in, cd_cmp_pos, cd_cmp_w1, cd_cmp_w2, hgrn_gamma, cd_head_norm, cd_w_out, ffn_w_up, ffn_conv_w, ffn_conv_b, ffn_w_down):
    depth = norm_mix.shape[0]
    sm = jax.nn.softmax(hgrn_gamma.astype(F32), axis=0)
    lb_all = jnp.cumsum(sm, axis=0) - sm[0]
    h = x
    for layer in range(depth):
        j = layer // 2
        if layer % 2 == 0:
            h, u = _layer_ab(h, norm_mix[layer], ab_w_in[j], ab_conv_w[j], ab_conv_b[j], ab_gate_b[j],
                             ab_head_norm[j], ab_w_out[j], norm_ffn[layer])
        else:
            h, u = _layer_cd(h, norm_mix[layer], cd_w_in[j], cd_cmp_pos[j], cd_cmp_w1[j], cd_cmp_w2[j],
                             lb_all[layer], cd_head_norm[j], cd_w_out[j], norm_ffn[layer])
        h = _ffn(h, u, ffn_w_up[layer], ffn_conv_w[layer], ffn_conv_b[layer], ffn_w_down[layer])
    return _rmsnorm(h, norm_final.reshape(1, -1), tm=min(512, h.shape[1]))
```

```python
import functools

import numpy as np
import jax
import jax.numpy as jnp
from jax import lax
from jax.experimental import pallas as pl
from jax.experimental.pallas import tpu as pltpu

F32 = jnp.float32
BF16 = jnp.bfloat16

HEAD_DIM = 128
N_HEADS = 8
KV_GROUPS = 2
GROUP_HEADS = N_HEADS // KV_GROUPS
A_CONV = 4
CMP_LEN = 32
CMP_STRIDE = 16
SEL_LEN = 64
SEL_SHIFT = 6
N_SEL = 8
N_LOCAL = 2
FORCE_BONUS = 1000.0
WINDOW = 256
FFN_CONV = 3
ROPE_THETA = 500000.0
ROPE_DIM = HEAD_DIM // 4
RMS_EPS = 1e-6
SCALE = HEAD_DIM ** -0.5
LOG2_E = float(np.log2(np.e))

CHUNK = 128
HEADS_PER_STEP = 8
HGRN_HEADS_PER_STEP = 8
CD_PROJ_COL_TILES = 3
HGRN_LEVELS = (64, 32, 16, 8, 4, 2, 1)
FFN_COL_TILES = 4
OUT_ROW_TILE = 512
OUT_COLS = 512
SB_SUB = 256
SB_ROW_GROUPS = 4
SB_DEAD_LOG2 = -160.0
PROJ_COL_TILES = 4
MXU_COLS = 256
VMEM_LIMIT = 56 * 1024 * 1024


def _cparams(sem):
    return pltpu.CompilerParams(dimension_semantics=sem, vmem_limit_bytes=VMEM_LIMIT)


def _dot(a, b):
    return jnp.dot(a, b, preferred_element_type=F32)


def _dot_nt(a, b):
    return lax.dot_general(a, b, (((1,), (1,)), ((), ())), preferred_element_type=F32)


def _dot_split(a, b01, terms):
    acc = None
    rem = a
    for _ in range(terms):
        piece = rem.astype(BF16)
        part = _dot(piece, b01)
        acc = part if acc is None else acc + part
        rem = rem - piece.astype(F32)
    return acc


def _dot_split_rhs(a01, b, terms):
    acc = None
    rem = b
    for _ in range(terms):
        piece = rem.astype(BF16)
        part = _dot(a01, piece)
        acc = part if acc is None else acc + part
        rem = rem - piece.astype(F32)
    return acc


def _dot_nt_split(a01, b, terms):
    acc = None
    rem = b
    for _ in range(terms):
        piece = rem.astype(BF16)
        part = _dot_nt(a01, piece)
        acc = part if acc is None else acc + part
        rem = rem - piece.astype(F32)
    return acc


def _softplus(z):
    return jnp.maximum(z, 0.0) + jnp.log(1.0 + jnp.exp(-jnp.abs(z)))


def _log_sigmoid(z):
    return -_softplus(-z)


def _sigmoid(z):
    return 1.0 / (1.0 + jnp.exp(-z))


def _iota(shape, dim):
    return lax.broadcasted_iota(jnp.int32, shape, dim)


def _rope_swap(x):
    lane = _iota(x.shape, x.ndim - 1)
    up = pltpu.roll(x, HEAD_DIM - ROPE_DIM // 2, axis=x.ndim - 1)
    down = pltpu.roll(x, ROPE_DIM // 2, axis=x.ndim - 1)
    return jnp.where(lane < ROPE_DIM // 2, up, down)


def _rope_tables(pos):
    half = ROPE_DIM // 2
    freqs = ROPE_THETA ** (-np.arange(half, dtype=np.float32) / half)
    ang = jnp.asarray(pos, F32)[:, None] * jnp.asarray(freqs, F32)[None, :]
    cos, sin = jnp.cos(ang), jnp.sin(ang)
    n = ang.shape[0]
    cos_t = jnp.concatenate([cos, cos, jnp.ones((n, HEAD_DIM - ROPE_DIM), F32)], axis=1)
    sin_t = jnp.concatenate([-sin, sin, jnp.zeros((n, HEAD_DIM - ROPE_DIM), F32)], axis=1)
    return cos_t, sin_t


def _rms_to_bf16(h_ref, g_ref):
    x = h_ref[...]
    ms = jnp.mean(x * x, axis=-1, keepdims=True)
    return (x * lax.rsqrt(ms + RMS_EPS) * g_ref[...]).astype(BF16)


def _norm_proj_kernel(*refs, gates_in_main):
    if gates_in_main:
        h_ref, g_ref, w_ref, o_ref, og_ref, u_scr = refs
    else:
        h_ref, g_ref, w_ref, wg_ref, o_ref, og_ref, u_scr = refs
    j = pl.program_id(2)

    @pl.when(j == 0)
    def _():
        u = _rms_to_bf16(h_ref, g_ref)
        u_scr[...] = u
        if not gates_in_main:
            og_ref[...] = _dot(u, wg_ref[...])

    u = u_scr[...]
    tn = w_ref.shape[1]
    for c0 in range(0, tn, MXU_COLS):
        width = min(MXU_COLS, tn - c0)
        res = _dot(u, w_ref[:, c0:c0 + width])
        for c in range(width // HEAD_DIM):
            o_ref[c0 // HEAD_DIM + c] = res[:, c * HEAD_DIM:(c + 1) * HEAD_DIM].astype(o_ref.dtype)
        if gates_in_main and c0 + width == tn:
            @pl.when(j == pl.num_programs(2) - 1)
            def _():
                og_ref[...] = res


def _norm_proj(h, g, w, wg, *, tm, tn, gates_in_main=False):
    b, s, d = h.shape
    ng = wg.shape[1]
    if gates_in_main:
        assert ng == MXU_COLS
        w = jnp.concatenate([w, wg], axis=1)
    n = w.shape[1]
    ncb = tn // HEAD_DIM
    operands = [h, g, w] + ([] if gates_in_main else [wg])
    return pl.pallas_call(
        functools.partial(_norm_proj_kernel, gates_in_main=gates_in_main),
        grid=(b, s // tm, n // tn),
        in_specs=[
            pl.BlockSpec((None, tm, d), lambda bi, si, j: (bi, si, 0)),
            pl.BlockSpec((1, d), lambda bi, si, j: (0, 0)),
            pl.BlockSpec((d, tn), lambda bi, si, j: (0, j)),
        ] + ([] if gates_in_main else [pl.BlockSpec((d, ng), lambda bi, si, j: (0, 0))]),
        out_specs=[
            pl.BlockSpec((None, ncb, tm, HEAD_DIM), lambda bi, si, j: (bi, j, si, 0)),
            pl.BlockSpec((None, tm, ng), lambda bi, si, j: (bi, si, 0)),
        ],
        out_shape=[
            jax.ShapeDtypeStruct((b, n // HEAD_DIM, s, HEAD_DIM), BF16),
            jax.ShapeDtypeStruct((b, s, ng), F32),
        ],
        scratch_shapes=[pltpu.VMEM((tm, d), BF16)],
        compiler_params=_cparams(("parallel", "parallel", "arbitrary")),
        name="norm_proj",
    )(*operands)


def _ffn_up_kernel(u_ref, wg_ref, wu_ref, cw_ref, cb_ref, o_ref, halo_scr):
    si = pl.program_id(1)
    j = pl.program_id(2)
    tm, tn = o_ref.shape

    @pl.when(si == 0)
    def _():
        halo_scr[j] = jnp.zeros(halo_scr.shape[1:], F32)

    u = u_ref[...]
    cw = cw_ref[...]
    cb = cb_ref[...]
    for c0 in range(0, tn, HEAD_DIM):
        cols = slice(c0, c0 + HEAD_DIM)
        res = _dot(u, jnp.concatenate([wg_ref[:, cols], wu_ref[:, cols]], axis=1))
        gate = res[:, :HEAD_DIM]
        up = res[:, HEAD_DIM:]
        ext = jnp.concatenate([halo_scr[j, :, cols], gate], axis=0)
        halo_scr[j, :, cols] = gate[tm - 8:, :]
        conv = cb[:, cols] + cw[FFN_CONV - 1:FFN_CONV, cols] * gate
        for k in range(FFN_CONV - 1):
            off = 8 - (FFN_CONV - 1) + k
            conv = conv + cw[k:k + 1, cols] * ext[off:off + tm, :]
        o_ref[:, cols] = (conv * _sigmoid(conv) * up).astype(o_ref.dtype)


def _ffn_up(u, w_up, cw, cb, *, tm, tn):
    b, s, d = u.shape
    f = w_up.shape[1] // 2
    nj = f // tn
    return pl.pallas_call(
        _ffn_up_kernel,
        grid=(b, s // tm, nj),
        in_specs=[
            pl.BlockSpec((None, tm, d), lambda bi, si, j: (bi, si, 0)),
            pl.BlockSpec((d, tn), lambda bi, si, j: (0, j)),
            pl.BlockSpec((d, tn), lambda bi, si, j: (0, nj + j)),
            pl.BlockSpec((FFN_CONV, tn), lambda bi, si, j: (0, j)),
            pl.BlockSpec((1, tn), lambda bi, si, j: (0, j)),
        ],
        out_specs=pl.BlockSpec((None, tm, tn), lambda bi, si, j: (bi, si, j)),
        out_shape=jax.ShapeDtypeStruct((b, s, f), BF16),
        scratch_shapes=[pltpu.VMEM((f // tn, 8, tn), F32)],
        compiler_params=_cparams(("parallel", "arbitrary", "arbitrary")),
        name="ffn_up",
    )(u, w_up, w_up, cw, cb)


def _out_proj_norm_kernel(h_ref, a1_ref, w1_ref, a2_ref, w2_ref, g_ref, ho_ref, u_ref):
    tm, d = ho_ref.shape
    a1 = a1_ref[...]
    a2 = a2_ref[...]
    ss = jnp.zeros((tm, 1), F32)
    for c0 in range(0, d, OUT_COLS):
        cols = slice(c0, c0 + OUT_COLS)
        acc = h_ref[:, cols] + _dot(a1, w1_ref[:, cols]) + _dot(a2, w2_ref[:, cols])
        ho_ref[:, cols] = acc
        ss = ss + jnp.sum(acc * acc, axis=1, keepdims=True)
    inv = lax.rsqrt(ss * (1.0 / d) + RMS_EPS)
    for c0 in range(0, d, OUT_COLS):
        cols = slice(c0, c0 + OUT_COLS)
        u_ref[:, cols] = (ho_ref[:, cols] * inv * g_ref[:, cols]).astype(u_ref.dtype)


def _out_proj_norm(h, a1, a2, w, g, *, tm):
    b, s, d = h.shape
    k = a1.shape[-1]
    assert a2.shape[-1] == k and w.shape[0] == 2 * k
    rows = lambda n: pl.BlockSpec((None, tm, n), lambda bi, si: (bi, si, 0))
    half = lambda i: pl.BlockSpec((k, d), lambda bi, si: (i, 0))
    return pl.pallas_call(
        _out_proj_norm_kernel,
        grid=(b, s // tm),
        in_specs=[rows(d), rows(k), half(0), rows(k), half(1), pl.BlockSpec((1, d), lambda bi, si: (0, 0))],
        out_specs=[rows(d), rows(d)],
        out_shape=[jax.ShapeDtypeStruct((b, s, d), F32), jax.ShapeDtypeStruct((b, s, d), BF16)],
        compiler_params=_cparams(("parallel", "parallel")),
        name="out_proj_norm",
    )(h, a1, w, a2, w, g)


def _proj_res_kernel(*refs):
    h_ref, o_ref = refs[0], refs[-1]
    ops = refs[1:-1]
    acc = h_ref[...]
    for i in range(0, len(ops), 2):
        acc = acc + _dot(ops[i][...], ops[i + 1][...])
    o_ref[...] = acc


def _proj_res(h, pairs, *, tm, tn):
    b, s, d = h.shape
    in_specs = [pl.BlockSpec((None, tm, tn), lambda bi, si, j: (bi, si, j))]
    args = [h]
    for a, w in pairs:
        k = a.shape[-1]
        in_specs.append(pl.BlockSpec((None, tm, k), lambda bi, si, j: (bi, si, 0)))
        in_specs.append(pl.BlockSpec((k, tn), lambda bi, si, j: (0, j)))
        args += [a, w]
    return pl.pallas_call(
        _proj_res_kernel,
        grid=(b, s // tm, d // tn),
        in_specs=in_specs,
        out_specs=pl.BlockSpec((None, tm, tn), lambda bi, si, j: (bi, si, j)),
        out_shape=jax.ShapeDtypeStruct((b, s, d), F32),
        compiler_params=_cparams(("parallel", "parallel", "arbitrary")),
        name="proj_res",
    )(*args)


def _rmsnorm_kernel(h_ref, g_ref, o_ref):
    x = h_ref[...]
    ms = jnp.mean(x * x, axis=-1, keepdims=True)
    o_ref[...] = x * lax.rsqrt(ms + RMS_EPS) * g_ref[...]


def _rmsnorm(h, g, *, tm):
    b, s, d = h.shape
    return pl.pallas_call(
        _rmsnorm_kernel,
        grid=(b, s // tm),
        in_specs=[pl.BlockSpec((None, tm, d), lambda bi, si: (bi, si, 0)),
                  pl.BlockSpec((1, d), lambda bi, si: (0, 0))],
        out_specs=pl.BlockSpec((None, tm, d), lambda bi, si: (bi, si, 0)),
        out_shape=jax.ShapeDtypeStruct((b, s, d), F32),
        compiler_params=_cparams(("parallel", "parallel")),
        name="final_norm",
    )(h, g)


def _mlstm_kernel(q_ref, k_ref, v_ref, og_ref, ig_ref, fg_ref, bi_ref, bf_ref,
                  cwq_ref, cwk_ref, cbq_ref, cbk_ref, hn_ref, out_ref,
                  pad_scr, qc_scr, kc_scr, c_scr, b_scr, d_scr):
    nh, s, _ = q_ref.shape
    nch = s // CHUNK

    def conv_silu(src_ref, cw_ref, cb_ref, dst_ref, hh, scale):
        lanes = slice(hh * HEAD_DIM, (hh + 1) * HEAD_DIM)
        pad_scr[0:8, :] = jnp.zeros((8, HEAD_DIM), F32)
        for p in range(nch):
            pad_scr[8 + p * CHUNK:8 + (p + 1) * CHUNK, :] = src_ref[hh, p * CHUNK:(p + 1) * CHUNK, :].astype(F32)
        cw = cw_ref[:, lanes]
        for p in range(nch):
            acc = cb_ref[:, lanes] + cw[0:1, :] * pad_scr[pl.ds(8 - (A_CONV - 1) + p * CHUNK, CHUNK), :]
            for j in range(1, A_CONV):
                acc = acc + cw[j:j + 1, :] * pad_scr[pl.ds(8 - (A_CONV - 1) + j + p * CHUNK, CHUNK), :]
            y = acc * _sigmoid(acc)
            if scale != 1.0:
                y = y * scale
            dst_ref[hh, p * CHUNK:(p + 1) * CHUNK, :] = y.astype(BF16)

    r_i = _iota((CHUNK, CHUNK), 0)
    c_i = _iota((CHUNK, CHUNK), 1)
    upper = (r_i <= c_i).astype(BF16)
    for hh in range(nh):
        conv_silu(q_ref, cwq_ref, cbq_ref, qc_scr, hh, 1.0)
        conv_silu(k_ref, cwk_ref, cbk_ref, kc_scr, hh, SCALE)
        log_f = _log_sigmoid(fg_ref[hh] + bf_ref[hh])
        log_i = ig_ref[hh] + bi_ref[hh]
        b_all = _dot_split(log_f, upper, 3)
        b_scr[hh] = b_all
        d_scr[hh] = log_i - b_all
    c_scr[...] = jnp.zeros(c_scr.shape, F32)
    causal = c_i <= r_i
    ones_blk = jnp.ones((CHUNK, HEAD_DIM), BF16)

    def head_chunk(hh, c, t0, m):
        lanes = slice(hh * HEAD_DIM, (hh + 1) * HEAD_DIM)
        q = qc_scr[hh, pl.ds(t0, CHUNK), :]
        k = kc_scr[hh, pl.ds(t0, CHUNK), :]
        v_aug = jnp.concatenate([v_ref[hh, pl.ds(t0, CHUNK), :], ones_blk], axis=1)
        b_row = b_scr[hh, pl.ds(c, 1), :]
        d_row = d_scr[hh, pl.ds(c, 1), :]
        b_col = jnp.transpose(jnp.broadcast_to(b_row, (CHUNK, CHUNK)))
        d_col = jnp.transpose(jnp.broadcast_to(d_row, (CHUNK, CHUNK)))
        dmat = jnp.where(causal, b_col + d_row, -jnp.inf)
        g = b_col + m
        m_row = jnp.maximum(g, jnp.max(dmat, axis=1, keepdims=True))
        w = jnp.exp(dmat - m_row) * _dot_nt(q, k)
        w_state = jnp.exp(g - m_row)
        c_old = c_scr[hh]
        inter = _dot(q, c_old.astype(BF16))
        intra = _dot(w.astype(BF16), v_aug)
        tot = jnp.concatenate([w_state, w_state], axis=1) * inter + intra
        num = tot[:, :HEAD_DIM]
        den = tot[:, HEAD_DIM:]
        hv = num / jnp.maximum(jnp.abs(den), jnp.exp(-m_row))
        y = hv * lax.rsqrt(jnp.mean(hv * hv, axis=-1, keepdims=True) + RMS_EPS) * hn_ref[:, lanes]
        y = y * _sigmoid(og_ref[hh, pl.ds(t0, CHUNK), :].astype(F32))
        out_ref[pl.ds(t0, CHUNK), lanes] = y.astype(out_ref.dtype)
        b_last = b_col[CHUNK - 1:CHUNK, :]
        w_new = b_last + d_col
        m_new = jnp.maximum(b_last + m, jnp.max(w_new, axis=0, keepdims=True))
        decay = jnp.exp(b_last + m - m_new)
        kw = jnp.exp(w_new - m_new) * k.astype(F32)
        kw_t = jnp.transpose(kw).astype(BF16)
        c_scr[hh] = jnp.concatenate([decay, decay], axis=1) * c_old + _dot(kw_t, v_aug)
        return m_new

    def body(c, ms):
        t0 = pl.multiple_of(c * CHUNK, CHUNK)
        return tuple(head_chunk(hh, c, t0, ms[hh]) for hh in range(nh))

    lax.fori_loop(0, nch, body, tuple(jnp.zeros((1, CHUNK), F32) for _ in range(nh)))


def _mlstm(p, gi, gf, bias_i, bias_f, cw, cb, hn):
    b, _, s, _ = p.shape
    nch = s // CHUNK
    nh = HEADS_PER_STEP
    nblk = N_HEADS // nh
    w = nh * HEAD_DIM
    slab = lambda off: pl.BlockSpec((None, nh, s, HEAD_DIM), lambda bi, hi: (bi, off + hi, 0, 0))
    gate = pl.BlockSpec((None, nh, nch, CHUNK), lambda bi, hi: (bi, hi, 0, 0))
    hb = pl.BlockSpec((nh, 1, CHUNK), lambda bi, hi: (hi, 0, 0))
    return pl.pallas_call(
        _mlstm_kernel,
        grid=(b, nblk),
        in_specs=[
            slab(0), slab(nblk), slab(2 * nblk), slab(3 * nblk), gate, gate, hb, hb,
            pl.BlockSpec((A_CONV, w), lambda bi, hi: (0, hi)),
            pl.BlockSpec((A_CONV, w), lambda bi, hi: (0, nblk + hi)),
            pl.BlockSpec((1, w), lambda bi, hi: (0, hi)),
            pl.BlockSpec((1, w), lambda bi, hi: (0, nblk + hi)),
            pl.BlockSpec((1, w), lambda bi, hi: (0, hi)),
        ],
        out_specs=pl.BlockSpec((None, s, w), lambda bi, hi: (bi, 0, hi)),
        out_shape=jax.ShapeDtypeStruct((b, s, N_HEADS * HEAD_DIM), BF16),
        scratch_shapes=[
            pltpu.VMEM((s + 8, HEAD_DIM), F32),
            pltpu.VMEM((nh, s, HEAD_DIM), BF16),
            pltpu.VMEM((nh, s, HEAD_DIM), BF16),
            pltpu.VMEM((nh, HEAD_DIM, 2 * HEAD_DIM), F32),
            pltpu.VMEM((nh, nch, CHUNK), F32),
            pltpu.VMEM((nh, nch, CHUNK), F32),
        ],
        compiler_params=_cparams(("parallel", "parallel")),
        name="mlstm",
    )(p, p, p, p, gi, gf, bias_i, bias_f, cw, cw, cb, cb, hn)


def _sb_kernel(q_ref, k_ref, v_ref, o_ref):
    qi = pl.program_id(2)
    n_rg = SB_ROW_GROUPS
    sub = q_ref.shape[0] // n_rg
    r_i = _iota((sub, sub), 0)
    c_i = _iota((sub, sub), 1)
    suffix = (r_i > c_i).astype(BF16)
    strict = c_i < r_i
    qs = [q_ref[g * sub:(g + 1) * sub, :] for g in range(n_rg)]
    j_diag = [n_rg * qi + g for g in range(n_rg)]

    def sub_block(q, j, run, diag):
        k0 = pl.multiple_of(j * sub, sub)
        k = k_ref[pl.ds(k0, sub), :]
        v = v_ref[pl.ds(k0, sub), :]
        z2 = _dot_nt(q, k)
        sp2 = jnp.maximum(z2, 0.0) + jnp.log2(1.0 + jnp.exp2(-jnp.abs(z2)))
        log_keep = -sp2
        if diag:
            log_keep = jnp.where(strict, log_keep, 0.0)
        later = _dot(log_keep.astype(BF16), suffix) + run
        a = jnp.exp2(z2 - sp2 + later)
        if diag:
            a = jnp.where(strict, a, 0.0)
        return _dot(a.astype(BF16), v), jnp.sum(log_keep, axis=1, keepdims=True)

    def any_alive(runs, n_next):
        top = jnp.full((sub, 1), -jnp.inf, F32)
        for g in range(n_rg):
            top = jnp.maximum(top, jnp.where(j_diag[g] >= n_next, runs[g], -jnp.inf))
        return (jnp.max(top) > SB_DEAD_LOG2).astype(jnp.int32)

    accs, runs = [], []
    for g in range(n_rg):
        o, sm = sub_block(qs[g], j_diag[g], 0.0, True)
        accs.append(o)
        runs.append(sm)

    def cond(carry):
        return carry[1] > 0

    def body(carry):
        n = carry[0]
        accs, runs = list(carry[2]), list(carry[3])
        for g in range(n_rg):
            j = j_diag[g] - n
            o, sm = sub_block(qs[g], jnp.maximum(j, 0), runs[g], False)
            live = j >= 0
            accs[g] = accs[g] + jnp.where(live, o, 0.0)
            runs[g] = runs[g] + jnp.where(live, sm, 0.0)
        return n + 1, any_alive(runs, n + 1), tuple(accs), tuple(runs)

    out = lax.while_loop(cond, body, (jnp.int32(1), any_alive(runs, 1), tuple(accs), tuple(runs)))
    for g in range(n_rg):
        o_ref[g * sub:(g + 1) * sub, :] = out[2][g].astype(o_ref.dtype)


def _stick_breaking(p, *, q_off, k_off, v_off, tq):
    b, _, s, _ = p.shape
    return pl.pallas_call(
        _sb_kernel,
        grid=(b, N_HEADS, s // tq),
        in_specs=[
            pl.BlockSpec((None, None, tq, HEAD_DIM), lambda bi, hi, qi: (bi, q_off + hi, qi, 0)),
            pl.BlockSpec((None, None, s, HEAD_DIM), lambda bi, hi, qi: (bi, k_off + hi, 0, 0)),
            pl.BlockSpec((None, None, s, HEAD_DIM), lambda bi, hi, qi: (bi, v_off + hi, 0, 0)),
        ],
        out_specs=pl.BlockSpec((None, tq, HEAD_DIM), lambda bi, hi, qi: (bi, qi, hi)),
        out_shape=jax.ShapeDtypeStruct((b, s, N_HEADS * HEAD_DIM), BF16),
        compiler_params=_cparams(("parallel", "parallel", "arbitrary")),
        name="stick_breaking",
    )(p, p, p)


def _gelu_tanh(x):
    return 0.5 * x * (1.0 + jnp.tanh(np.sqrt(2.0 / np.pi).astype(np.float32) * (x + 0.044715 * (x * x * x))))


def _nsa_prep_kernel(kcs_ref, vcs_ref, ks_ref, kw_ref, pos_ref, w1_ref, w2_ref,
                     cos_ref, sin_ref, cosc_ref, sinc_ref,
                     kc_out, vc_out, ks_out, kw_out, f32_scr):
    s = ks_ref.shape[0]
    ng = s // CMP_STRIDE

    def compress(src_ref, idx):
        f32_scr[...] = src_ref[...].astype(F32)
        xa = jnp.zeros((ng, HEAD_DIM), F32)
        xb = jnp.zeros((ng, HEAD_DIM), F32)
        for l in range(CMP_STRIDE):
            rows = f32_scr[pl.ds(l, ng, stride=CMP_STRIDE), :]
            xa = xa + _dot((rows + pos_ref[idx, l:l + 1, :]).astype(BF16), w1_ref[idx, l])
            xb = xb + _dot((rows + pos_ref[idx, CMP_STRIDE + l:CMP_STRIDE + l + 1, :]).astype(BF16),
                           w1_ref[idx, CMP_STRIDE + l])
        hid = _gelu_tanh(xa + pltpu.roll(xb, ng - 1, axis=0))
        return _dot(hid.astype(BF16), w2_ref[idx])

    kc = compress(kcs_ref, 0)
    kc_out[...] = (kc * cosc_ref[...] + _rope_swap(kc) * sinc_ref[...]).astype(kc_out.dtype)
    vc_out[...] = compress(vcs_ref, 1).astype(vc_out.dtype)
    for src, dst in ((ks_ref, ks_out), (kw_ref, kw_out)):
        x = src[...].astype(F32)
        dst[...] = (x * cos_ref[...] + _rope_swap(x) * sin_ref[...]).astype(dst.dtype)


def _nsa_prep(p, pos, w1, w2, cos, sin, cosc, sinc, *, kc_off, vc_off, ks_off, kw_off):
    b, _, s, _ = p.shape
    ng = s // CMP_STRIDE
    slab = lambda off: pl.BlockSpec((None, None, s, HEAD_DIM), lambda bi, gi: (bi, off + gi, 0, 0))
    full = lambda a: pl.BlockSpec(a.shape, lambda bi, gi: (0,) * a.ndim)
    out_c = pl.BlockSpec((None, None, ng, HEAD_DIM), lambda bi, gi: (bi, gi, 0, 0))
    out_s = pl.BlockSpec((None, None, s, HEAD_DIM), lambda bi, gi: (bi, gi, 0, 0))
    return pl.pallas_call(
        _nsa_prep_kernel,
        grid=(b, KV_GROUPS),
        in_specs=[slab(kc_off), slab(vc_off), slab(ks_off), slab(kw_off),
                  full(pos), full(w1), full(w2), full(cos), full(sin), full(cosc), full(sinc)],
        out_specs=[out_c, out_c, out_s, out_s],
        out_shape=[jax.ShapeDtypeStruct((b, KV_GROUPS, ng, HEAD_DIM), BF16),
                   jax.ShapeDtypeStruct((b, KV_GROUPS, ng, HEAD_DIM), BF16),
                   jax.ShapeDtypeStruct((b, KV_GROUPS, s, HEAD_DIM), BF16),
                   jax.ShapeDtypeStruct((b, KV_GROUPS, s, HEAD_DIM), BF16)],
        scratch_shapes=[pltpu.VMEM((s, HEAD_DIM), F32)],
        compiler_params=_cparams(("parallel", "parallel")),
        name="nsa_prep",
    )(p, p, p, p, pos, w1, w2, cos, sin, cosc, sinc)


def _masked_softmax(sc, mask):
    sc = jnp.where(mask, sc, -jnp.inf)
    m = jnp.max(sc, axis=-1, keepdims=True)
    m = jnp.where(m > -jnp.inf, m, 0.0)
    e = jnp.exp(sc - m)
    return e / jnp.maximum(jnp.sum(e, axis=-1, keepdims=True), 1e-30)


def _nsa_kernel(q_ref, kc_ref, vc_ref, ks_ref, vs_ref, kw_ref, vw_ref, gate_ref, cos_ref, sin_ref,
                cover_t_ref, o_ref, *, tk):
    qi = pl.program_id(2)
    r = q_ref.shape[0]
    tq = q_ref.shape[1]
    s_len = ks_ref.shape[0]
    n_cmp = (s_len - CMP_LEN) // CMP_STRIDE + 1
    t0 = qi * tq

    qf = q_ref[...].astype(F32).reshape(r * tq, HEAD_DIM)
    cos = jnp.concatenate([cos_ref[...]] * r, axis=0)
    sin = jnp.concatenate([sin_ref[...]] * r, axis=0)
    qb = (qf * cos + _rope_swap(qf) * sin).astype(BF16)

    t_col = t0 + _iota((tq, HEAD_DIM), 0)
    lane = _iota((tq, HEAD_DIM), 1)

    ncl = kc_ref.shape[0]
    sc = _dot_nt(qb, kc_ref[...]).reshape(r, tq, ncl)
    cmp_ok = ((lane * CMP_STRIDE + (CMP_LEN - 1) <= t_col) & (lane < n_cmp))[:, :ncl]
    p_cmp = _masked_softmax(sc, cmp_ok[None])
    o_cmp = _dot(p_cmp.reshape(r * tq, ncl).astype(BF16), vc_ref[...]).reshape(r, tq, HEAD_DIM)

    band = WINDOW + tq
    w0 = pl.multiple_of(jnp.maximum(t0 - WINDOW, 0), tq)
    k_win = kw_ref[pl.ds(w0, band), :]
    v_win = vw_ref[pl.ds(w0, band), :]
    diff = (t0 + _iota((tq, band), 0)) - (w0 + _iota((tq, band), 1))
    win_ok = (diff >= 0) & (diff < WINDOW)
    sw = _dot_nt(qb, k_win).reshape(r, tq, band)
    p_win = _masked_softmax(sw, win_ok[None])
    o_win = _dot(p_win.reshape(r * tq, band).astype(BF16), v_win).reshape(r, tq, HEAD_DIM)

    p_sum = p_cmp[0]
    for h in range(1, r):
        p_sum = p_sum + p_cmp[h]
    n_blk = s_len // SEL_LEN
    p_slc_t = _dot_nt_split(cover_t_ref[...], p_sum, 3)[:n_blk, :]
    blk = _iota((n_blk, tq), 0)
    blk_t = (t0 + _iota((n_blk, tq), 1)) >> SEL_SHIFT
    started = blk <= blk_t
    forced = (blk == 0) | (started & (blk_t - blk < N_LOCAL))
    score = jnp.where(started, p_slc_t + jnp.where(forced, FORCE_BONUS, 0.0), -jnp.inf)
    rank = jnp.zeros((n_blk, tq), F32)
    for jp in range(n_blk):
        row = score[jp:jp + 1, :]
        beats = (row > score) | ((row == score) & (blk > jp))
        rank = rank + jnp.where(beats, 1.0, 0.0)
    sel_t = jnp.where((rank < N_SEL) & started, 1.0, 0.0)
    sel_t = jnp.concatenate([sel_t, jnp.zeros((HEAD_DIM - n_blk, tq), F32)], axis=0)
    sel = jnp.transpose(sel_t).astype(BF16)

    e_row = _iota((HEAD_DIM, tk), 0)
    e_blk = _iota((HEAD_DIM, tk), 1) >> SEL_SHIFT
    s_loc = _iota((tq, tk), 1)
    t_q = t0 + _iota((tq, tk), 0)
    n_chain = r
    hc = r // n_chain
    q_chain = [qb[c * hc * tq:(c + 1) * hc * tq, :] for c in range(n_chain)]
    ones_blk = jnp.ones((tk, HEAD_DIM), BF16)

    def sel_tile(kt, carry):
        k0 = pl.multiple_of(kt * tk, tk)
        k = ks_ref[pl.ds(k0, tk), :]
        v_aug = jnp.concatenate([vs_ref[pl.ds(k0, tk), :], ones_blk], axis=1)
        expand = (e_row == e_blk + kt * (tk // SEL_LEN)).astype(BF16)
        ok = (_dot(sel, expand) > 0.5) & (k0 + s_loc <= t_q)
        bias = jnp.where(ok, 0.0, -jnp.inf)[None]
        new = []
        for c in range(n_chain):
            m, acc = carry[c]
            sc_t = _dot_nt(q_chain[c], k).reshape(hc, tq, tk) + bias
            m_new = jnp.maximum(m, jnp.max(sc_t, axis=-1, keepdims=True))
            alpha = jnp.exp(m - m_new)
            e = jnp.exp(sc_t - m_new)
            pv = _dot(e.reshape(hc * tq, tk).astype(BF16), v_aug).reshape(hc, tq, 2 * HEAD_DIM)
            new.append((m_new, alpha * acc + pv))
        return tuple(new)

    n_tiles = (t0 + tq + tk - 1) // tk
    init = tuple((jnp.full((hc, tq, 1), -jnp.inf, F32), jnp.zeros((hc, tq, 2 * HEAD_DIM), F32))
                 for _ in range(n_chain))
    chains = lax.fori_loop(0, n_tiles, sel_tile, init)

    gates = _sigmoid(gate_ref[...])
    for h in range(r):
        acc_s = chains[h // hc][1][h % hc]
        o_slc = acc_s[:, :HEAD_DIM] / jnp.maximum(acc_s[:, HEAD_DIM:], 1e-30)
        out = (gates[:, 3 * h:3 * h + 1] * o_cmp[h] + gates[:, 3 * h + 1:3 * h + 2] * o_slc
               + gates[:, 3 * h + 2:3 * h + 3] * o_win[h])
        o_ref[:, h * HEAD_DIM:(h + 1) * HEAD_DIM] = out.astype(o_ref.dtype)


def _nsa(p, kc, vc, ks, kw, gates, cos, sin, cover, *, q_off, vs_off, vw_off, tq, tk):
    b, _, s, _ = p.shape
    r = GROUP_HEADS
    ng = kc.shape[2]
    kv = lambda: pl.BlockSpec((None, None, s, HEAD_DIM), lambda bi, gi, qi: (bi, gi, 0, 0))
    pslab = lambda off: pl.BlockSpec((None, None, s, HEAD_DIM), lambda bi, gi, qi: (bi, off + gi, 0, 0))
    cmp_spec = pl.BlockSpec((None, None, ng, HEAD_DIM), lambda bi, gi, qi: (bi, gi, 0, 0))
    return pl.pallas_call(
        functools.partial(_nsa_kernel, tk=tk),
        grid=(b, KV_GROUPS, s // tq),
        in_specs=[
            pl.BlockSpec((None, r, tq, HEAD_DIM), lambda bi, gi, qi: (bi, q_off // r + gi, qi, 0)),
            cmp_spec, cmp_spec, kv(), pslab(vs_off), kv(), pslab(vw_off),
            pl.BlockSpec((None, tq, HEAD_DIM), lambda bi, gi, qi: (bi, qi, gi)),
            pl.BlockSpec((tq, HEAD_DIM), lambda bi, gi, qi: (qi, 0)),
            pl.BlockSpec((tq, HEAD_DIM), lambda bi, gi, qi: (qi, 0)),
            pl.BlockSpec(cover.shape, lambda bi, gi, qi: (0, 0)),
        ],
        out_specs=pl.BlockSpec((None, tq, r * HEAD_DIM), lambda bi, gi, qi: (bi, qi, gi)),
        out_shape=jax.ShapeDtypeStruct((b, s, N_HEADS * HEAD_DIM), BF16),
        compiler_params=_cparams(("parallel", "parallel", "arbitrary")),
        name="nsa",
    )(p, kc, vc, ks, p, kw, p, gates, cos, sin, cover)


def _hgrn_tables():
    r = np.arange(CHUNK)[:, None]
    j = np.arange(CHUNK)[None, :]
    mats = []
    for sz in HGRN_LEVELS:
        mid = (r // (2 * sz)) * (2 * sz) + sz
        right = (r // sz) % 2 == 1
        mats.append(np.where(right, (j >= mid) & (j <= r), (j > r) & (j <= mid - 1)))
    mats.append(j <= r)
    mats.append(j > r)
    return np.concatenate(mats, axis=0).astype(np.float32)


def _hgrn_kernel(q_ref, f_ref, i_ref, g_ref, lb_ref, hn_ref, tab_ref, out_ref, st_scr):
    nh, s, _ = q_ref.shape
    nch = s // CHUNK
    nlev = len(HGRN_LEVELS)
    r_i = _iota((CHUNK, CHUNK), 0)
    c_i = _iota((CHUNK, CHUNK), 1)
    row = _iota((CHUNK, HEAD_DIM), 0)
    st_scr[...] = jnp.zeros(st_scr.shape, F32)

    def head_chunk(hh, t0):
        lanes = slice(hh * HEAD_DIM, (hh + 1) * HEAD_DIM)
        lb = lb_ref[:, lanes]
        q = q_ref[hh, pl.ds(t0, CHUNK), :].astype(F32)
        fp = f_ref[hh, pl.ds(t0, CHUNK), :].astype(F32)
        v = i_ref[hh, pl.ds(t0, CHUNK), :]
        log_lb = jnp.log(lb)
        a2 = jnp.log1p(-lb) - _softplus(-fp)
        hi = jnp.maximum(log_lb, a2)
        lo = jnp.minimum(log_lb, a2)
        log_f = hi + jnp.log(1.0 + jnp.exp(lo - hi))
        k = (1.0 - lb) * _sigmoid(-fp)
        x_all = _dot_split_rhs(tab_ref[...], log_f, 2)
        att = jnp.where(r_i == c_i, jnp.sum(q * k, axis=1, keepdims=True), 0.0)
        for lv, sz in enumerate(HGRN_LEVELS):
            sh = sz.bit_length() - 1
            e = jnp.exp(x_all[lv * CHUNK:(lv + 1) * CHUNK, :])
            right = ((row >> sh) & 1) == 1
            qt = jnp.where(right, q * e, 0.0).astype(BF16)
            kt = jnp.where(right, 0.0, k * e).astype(BF16)
            same = (r_i >> (sh + 1)) == (c_i >> (sh + 1))
            att = att + jnp.where(same, _dot_nt(qt, kt), 0.0)
        bc = x_all[nlev * CHUNK:(nlev + 1) * CHUNK, :]
        rev = x_all[(nlev + 1) * CHUNK:(nlev + 2) * CHUNK, :]
        st = st_scr[hh]
        o = _dot(att.astype(BF16), v) + _dot_nt((q * jnp.exp(bc)).astype(BF16), st.astype(BF16))
        y = o * lax.rsqrt(jnp.mean(o * o, axis=-1, keepdims=True) + RMS_EPS) * hn_ref[:, lanes]
        gg = g_ref[hh, pl.ds(t0, CHUNK), :].astype(F32)
        out_ref[pl.ds(t0, CHUNK), lanes] = (y * (gg * _sigmoid(gg))).astype(out_ref.dtype)
        v_t = jnp.transpose(v.astype(F32)).astype(BF16)
        st_scr[hh] = st * jnp.exp(bc[CHUNK - 1:CHUNK, :]) + _dot(v_t, (k * jnp.exp(rev)).astype(BF16))

    def body(c, carry):
        t0 = pl.multiple_of(c * CHUNK, CHUNK)
        for hh in range(nh):
            head_chunk(hh, t0)
        return carry

    lax.fori_loop(0, nch, body, 0)


def _hgrn(p, lb, hn, tab, *, q_off, f_off, i_off, g_off):
    b, _, s, _ = p.shape
    nh = HGRN_HEADS_PER_STEP
    nblk = N_HEADS // nh
    w = nh * HEAD_DIM
    slab = lambda off: pl.BlockSpec((None, nh, s, HEAD_DIM), lambda bi, hi: (bi, off // nh + hi, 0, 0))
    return pl.pallas_call(
        _hgrn_kernel,
        grid=(b, nblk),
        in_specs=[slab(q_off), slab(f_off), slab(i_off), slab(g_off),
                  pl.BlockSpec((1, w), lambda bi, hi: (0, hi)),
                  pl.BlockSpec((1, w), lambda bi, hi: (0, hi)),
                  pl.BlockSpec(tab.shape, lambda bi, hi: (0, 0))],
        out_specs=pl.BlockSpec((None, s, w), lambda bi, hi: (bi, 0, hi)),
        out_shape=jax.ShapeDtypeStruct((b, s, N_HEADS * HEAD_DIM), BF16),
        scratch_shapes=[pltpu.VMEM((nh, HEAD_DIM, HEAD_DIM), F32)],
        compiler_params=_cparams(("parallel", "parallel")),
        name="hgrn2",
    )(p, p, p, p, lb, hn, tab)


def _pad_cols(w, n):
    return jnp.pad(w, ((0, 0), (0, n - w.shape[1])))


def _row_tile(s):
    return min(1024, s)


def _layer_ab(h, norm_g, w_in, conv_w, conv_b, gate_b, head_norm, w_out, ffn_norm_g):
    b, s, d = h.shape
    aw = N_HEADS * HEAD_DIM
    tm = _row_tile(s)
    g0 = 4 * aw
    g1 = g0 + 2 * N_HEADS
    w_main = jnp.concatenate([w_in[:, :g0], w_in[:, g1:g1 + aw] * (SCALE * LOG2_E), w_in[:, g1 + aw:]],
                             axis=1).astype(BF16)
    w_gate = _pad_cols(w_in[:, g0:g1], HEAD_DIM).astype(BF16)
    p, gates = _norm_proj(h, norm_g.reshape(1, d), w_main, w_gate, tm=tm, tn=w_main.shape[1] // PROJ_COL_TILES)
    nch = s // CHUNK
    g_t = jnp.transpose(gates[:, :, :2 * N_HEADS], (0, 2, 1))
    gi = g_t[:, :N_HEADS].reshape(b, N_HEADS, nch, CHUNK)
    gf = g_t[:, N_HEADS:].reshape(b, N_HEADS, nch, CHUNK)
    bias_i = jnp.broadcast_to(gate_b[:N_HEADS, None, None], (N_HEADS, 1, CHUNK)).astype(F32)
    bias_f = jnp.broadcast_to(gate_b[N_HEADS:, None, None], (N_HEADS, 1, CHUNK)).astype(F32)
    h_a = _mlstm(p, gi, gf, bias_i, bias_f, conv_w, conv_b.reshape(1, -1), head_norm.reshape(1, -1))
    h_b = _stick_breaking(p, q_off=4 * N_HEADS, k_off=5 * N_HEADS, v_off=6 * N_HEADS, tq=min(SB_ROW_GROUPS * SB_SUB, s))
    w_o = w_out.astype(BF16)
    return _out_proj_norm(h, h_a, h_b, w_o, ffn_norm_g.reshape(1, d), tm=min(OUT_ROW_TILE, s))


def _layer_cd(h, norm_g, w_in, cmp_pos, cmp_w1, cmp_w2, lower_bound, head_norm, w_out, ffn_norm_g):
    b, s, d = h.shape
    cw = N_HEADS * HEAD_DIM
    kvw = KV_GROUPS * HEAD_DIM
    tm = _row_tile(s)
    n_gate = 3 * N_HEADS
    g0 = cw + 6 * kvw
    w_main = jnp.concatenate([w_in[:, g0 + n_gate:], w_in[:, :cw] * SCALE, w_in[:, cw:g0]], axis=1).astype(BF16)
    per_group = 3 * GROUP_HEADS
    w_gate = jnp.concatenate(
        [_pad_cols(w_in[:, g0 + g * per_group:g0 + (g + 1) * per_group], HEAD_DIM) for g in range(KV_GROUPS)],
        axis=1).astype(BF16)
    p, gates = _norm_proj(h, norm_g.reshape(1, d), w_main, w_gate, tm=tm, gates_in_main=True,
                          tn=(w_main.shape[1] + w_gate.shape[1]) // CD_PROJ_COL_TILES)
    pos = cmp_pos.astype(F32)
    w1 = cmp_w1.astype(BF16)
    w2 = cmp_w2.astype(BF16)
    cos, sin = _rope_tables(np.arange(s))
    ng = s // CMP_STRIDE
    cosc, sinc = _rope_tables(np.arange(ng) * CMP_STRIDE + CMP_LEN - 1)
    kc, vc, ks, kw = _nsa_prep(p, pos, w1, w2, cos, sin, cosc, sinc,
                               kc_off=40, vc_off=42, ks_off=44, kw_off=48)
    n_cmp = (s - CMP_LEN) // CMP_STRIDE + 1
    n_sel = s // SEL_LEN
    cs = np.arange(n_cmp) * CMP_STRIDE
    ss = np.arange(n_sel) * SEL_LEN
    cover_t = np.zeros((HEAD_DIM, ng), np.float32)
    cover_t[:n_sel, :n_cmp] = (cs[None, :] < ss[:, None] + SEL_LEN) & (cs[None, :] + CMP_LEN > ss[:, None])
    o_c = _nsa(p, kc, vc, ks, kw, gates, cos, sin, jnp.asarray(cover_t, BF16),
               q_off=32, vs_off=46, vw_off=50, tq=256, tk=min(512, s))
    o_d = _hgrn(p, lower_bound.reshape(1, -1).astype(F32), head_norm.reshape(1, -1),
                jnp.asarray(_hgrn_tables(), BF16), q_off=0, f_off=8, i_off=16, g_off=24)
    w_o = w_out.astype(BF16)
    return _out_proj_norm(h, o_c, o_d, w_o, ffn_norm_g.reshape(1, d), tm=min(OUT_ROW_TILE, s))


def _ffn(h, u, w_up, conv_w, conv_b, w_down):
    b, s, d = h.shape
    f = w_down.shape[0]
    tm = _row_tile(s)
    act = _ffn_up(u, w_up.astype(BF16), conv_w, conv_b.reshape(1, f), tm=tm, tn=f // FFN_COL_TILES)
    return _proj_res(h, [(act, w_down.astype(BF16))], tm=tm, tn=512)


def kernel(x, norm_mix, norm_ffn, norm_final, ab_w_in, ab_conv_w, ab_conv_b, ab_gate_b, ab_head_norm, ab_w_out, cd_w_in, cd_cmp_pos, cd_cmp_w1, cd_cmp_w2, hgrn_gamma, cd_head_norm, cd_w_out, ffn_w_up, ffn_conv_w, ffn_conv_b, ffn_w_down):
    depth = norm_mix.shape[0]
    sm = jax.nn.softmax(hgrn_gamma.astype(F32), axis=0)
    lb_all = jnp.cumsum(sm, axis=0) - sm[0]
    h = x
    for layer in range(depth):
        j = layer // 2
        if layer % 2 == 0:
            h, u = _layer_ab(h, norm_mix[layer], ab_w_in[j], ab_conv_w[j], ab_conv_b[j], ab_gate_b[j],
                             ab_head_norm[j], ab_w_out[j], norm_ffn[layer])
        else:
            h, u = _layer_cd(h, norm_mix[layer], cd_w_in[j], cd_cmp_pos[j], cd_cmp_w1[j], cd_cmp_w2[j],
                             lb_all[layer], cd_head_norm[j], cd_w_out[j], norm_ffn[layer])
        h = _ffn(h, u, ffn_w_up[layer], ffn_conv_w[layer], ffn_conv_b[layer], ffn_w_down[layer])
    return _rmsnorm(h, norm_final.reshape(1, -1), tm=min(512, h.shape[1]))
```
